```python
import math
import jax, jax.numpy as jnp
from jax import lax
import numpy as np

D_MODEL = 2048
BATCH = 16
SEQ = 2048
DEPTH = 2

D_FF = 4 * D_MODEL
GMLP_CHUNK = 128
GMLP_WIDTH = 2 * D_MODEL
GMLP_GROUPS = 16
GMLP_GROUP_DIM = GMLP_WIDTH // GMLP_GROUPS
HEAD_DIM = 128
N_HEADS = D_MODEL // HEAD_DIM
N_KV_GROUPS = 4
HEADS_PER_GROUP = N_HEADS // N_KV_GROUPS
N_BRANCHES = 3
CMP_BLOCK = 32
CMP_STRIDE = 16
CMP_HIDDEN = 256
SEL_BLOCK = 64
N_SELECT = 8
SEL_QUERY_BLOCK = 32
WINDOW = 512
WIN_QUERY_BLOCK = 128
ROPE_THETA = 10000.0
EPS = 1e-6
NEG_INF = -1e30
FORCE_BONUS = 1e6

kernel_name = "yoco_gmlp_nsa_hybrid"


def rms_norm(x, g):
    x32 = x.astype(jnp.float32)
    y = x32 * lax.rsqrt(jnp.mean(x32 * x32, axis=-1, keepdims=True) + EPS)
    return y.astype(x.dtype) * g


def layer_norm(x, g, b):
    x32 = x.astype(jnp.float32)
    mu = jnp.mean(x32, axis=-1, keepdims=True)
    var = jnp.mean(jnp.square(x32 - mu), axis=-1, keepdims=True)
    return ((x32 - mu) * lax.rsqrt(var + EPS)).astype(x.dtype) * g + b


def modulate(h, shift, scale):
    return h * (1.0 + scale[:, None, :]) + shift[:, None, :]


def rope(x, pos):
    half = HEAD_DIM // 2
    freqs = ROPE_THETA ** (-jnp.arange(half, dtype=jnp.float32) / half)
    ang = pos.astype(jnp.float32)[:, None] * freqs[None, :]
    cos = jnp.cos(ang)[:, None, :].astype(x.dtype)
    sin = jnp.sin(ang)[:, None, :].astype(x.dtype)
    x1, x2 = x[..., :half], x[..., half:]
    return jnp.concatenate([x1 * cos - x2 * sin, x2 * cos + x1 * sin], axis=-1)


def masked_softmax(s, mask, axis):
    s = jnp.where(mask, s.astype(jnp.float32), NEG_INF)
    m = jnp.max(s, axis=axis, keepdims=True)
    e = jnp.where(mask, jnp.exp(s - m), 0.0)
    den = jnp.sum(e, axis=axis, keepdims=True)
    return e / jnp.maximum(den, 1e-30)


def sq_relu_mlp(h, w1, w2):
    return jnp.square(jax.nn.relu(h @ w1)) @ w2


def gmlp_mixer(h, w_in, b_in, ln_g, ln_b, w_s, b_s, w_out, b_out):
    B, S, _ = h.shape
    z = jax.nn.gelu(h @ w_in + b_in)
    u, v = jnp.split(z, 2, axis=-1)
    v = layer_norm(v, ln_g, ln_b)
    v = v.reshape(B, S // GMLP_CHUNK, GMLP_CHUNK, GMLP_GROUPS, GMLP_GROUP_DIM)
    causal = jnp.tril(jnp.ones((GMLP_CHUNK, GMLP_CHUNK), dtype=bool))
    w_causal = jnp.where(causal[None], w_s, 0.0)
    v = jnp.einsum('gts,bnsgc->bntgc', w_causal, v) + b_s.T[None, None, :, :, None]
    return (u * v.reshape(B, S, GMLP_WIDTH)) @ w_out + b_out


def compress_blocks(x, pos_emb, w1, w2):
    B, S, G, d = x.shape
    halves = x.reshape(B, S // CMP_STRIDE, CMP_STRIDE, G, d)
    blocks = jnp.concatenate([halves[:, :-1], halves[:, 1:]], axis=2)
    blocks = blocks + pos_emb[None, None, :, None, :]
    flat = jnp.moveaxis(blocks, 2, 3).reshape(B, blocks.shape[1], G, CMP_BLOCK * d)
    return jax.nn.gelu(flat @ w1) @ w2


def nsa_shared_kv(h, w_kv, cmp_pos_k, cmp_w1_k, cmp_w2_k, cmp_pos_v, cmp_w1_v, cmp_w2_v):
    B, S, _ = h.shape
    pos = jnp.arange(S)
    kv = (h @ w_kv).reshape(B, S, 2 * N_BRANCHES, N_KV_GROUPS, HEAD_DIM)
    k_cmp, v_cmp = kv[:, :, 0], kv[:, :, 1]
    k_slc, v_slc = kv[:, :, 2], kv[:, :, 3]
    k_win, v_win = kv[:, :, 4], kv[:, :, 5]
    k_cmp = compress_blocks(rope(k_cmp, pos), cmp_pos_k, cmp_w1_k, cmp_w2_k)
    v_cmp = compress_blocks(v_cmp, cmp_pos_v, cmp_w1_v, cmp_w2_v)
    return k_cmp, v_cmp, rope(k_slc, pos), v_slc, rope(k_win, pos), v_win


def nsa_mixer(h, k_cmp, v_cmp, k_slc, v_slc, k_win, v_win, w_qg, w_o):
    B, S, _ = h.shape
    G, HG, d = N_KV_GROUPS, HEADS_PER_GROUP, HEAD_DIM
    pos = jnp.arange(S)
    qg = h @ w_qg
    q = qg[..., :N_HEADS * d].reshape(B, S, N_HEADS, d)
    gates = jax.nn.sigmoid(qg[..., N_HEADS * d:].astype(jnp.float32))
    gates = gates.reshape(B, S, G, HG, N_BRANCHES).astype(h.dtype)
    q = (rope(q, pos) * (d ** -0.5)).reshape(B, S, G, HG, d)

    n_cmp = k_cmp.shape[1]
    s_cmp = jnp.einsum('btghd,bngd->bghtn', q, k_cmp)
    cmp_end = jnp.arange(n_cmp) * CMP_STRIDE + CMP_BLOCK - 1
    p_cmp = masked_softmax(s_cmp, cmp_end[None, :] <= pos[:, None], axis=-1)
    o_cmp = jnp.einsum('bghtn,bngd->btghd', p_cmp.astype(h.dtype), v_cmp)

    n_slc = S // SEL_BLOCK
    ci = jnp.arange(n_cmp)[:, None]
    sj = jnp.arange(n_slc)[None, :]
    overlap = ((ci * CMP_STRIDE <= sj * SEL_BLOCK + SEL_BLOCK - 1)
               & (ci * CMP_STRIDE + CMP_BLOCK - 1 >= sj * SEL_BLOCK)).astype(jnp.float32)
    imp = jnp.einsum('bghtn,nj->bgtj', p_cmp, overlap)
    blk = jnp.arange(n_slc)[None, :]
    cur = (pos // SEL_BLOCK)[:, None]
    forced = (blk == 0) | (blk == cur) | (blk == cur - 1)
    imp = jnp.where(blk * SEL_BLOCK <= pos[:, None],
                    imp + jnp.where(forced, FORCE_BONUS, 0.0), NEG_INF)
    n_sel = min(N_SELECT, n_slc)
    _, sel_idx = lax.top_k(imp, n_sel)

    k_blk = k_slc.reshape(B, n_slc, SEL_BLOCK, G, d).transpose(0, 3, 1, 2, 4)
    v_blk = v_slc.reshape(B, n_slc, SEL_BLOCK, G, d).transpose(0, 3, 1, 2, 4)
    nq = S // SEL_QUERY_BLOCK
    q_sel = jnp.moveaxis(q.reshape(B, nq, SEL_QUERY_BLOCK, G, HG, d), 1, 0)
    idx_sel = jnp.moveaxis(sel_idx.reshape(B, G, nq, SEL_QUERY_BLOCK, n_sel), 2, 0)
    b_ix = jnp.arange(B)[:, None, None, None]
    g_ix = jnp.arange(G)[None, :, None, None]

    def sel_block(args):
        qb, ib, start = args
        kg = k_blk[b_ix, g_ix, ib]
        vg = v_blk[b_ix, g_ix, ib]
        s = jnp.einsum('btghd,bgtsrd->bghtsr', qb, kg)
        tpos = start + jnp.arange(SEL_QUERY_BLOCK)
        kpos = ib[..., None] * SEL_BLOCK + jnp.arange(SEL_BLOCK)
        mask = kpos <= tpos[None, None, :, None, None]
        p = masked_softmax(s, mask[:, :, None], axis=(-2, -1))
        return jnp.einsum('bghtsr,bgtsrd->btghd', p.astype(qb.dtype), vg)

    o_slc = lax.map(sel_block, (q_sel, idx_sel, jnp.arange(nq) * SEL_QUERY_BLOCK))
    o_slc = jnp.moveaxis(o_slc, 0, 1).reshape(B, S, G, HG, d)

    nw = S // WIN_QUERY_BLOCK
    span = WINDOW + WIN_QUERY_BLOCK
    k_pad = jnp.pad(k_win, ((0, 0), (WINDOW, 0), (0, 0), (0, 0)))
    v_pad = jnp.pad(v_win, ((0, 0), (WINDOW, 0), (0, 0), (0, 0)))
    q_win = jnp.moveaxis(q.reshape(B, nw, WIN_QUERY_BLOCK, G, HG, d), 1, 0)

    def win_block(args):
        qb, start = args
        kb = lax.dynamic_slice_in_dim(k_pad, start, span, axis=1)
        vb = lax.dynamic_slice_in_dim(v_pad, start, span, axis=1)
        s = jnp.einsum('btghd,bkgd->bghtk', qb, kb)
        tpos = (start + jnp.arange(WIN_QUERY_BLOCK))[:, None]
        kpos = (start - WINDOW + jnp.arange(span))[None, :]
        mask = (kpos <= tpos) & (kpos > tpos - WINDOW) & (kpos >= 0)
        p = masked_softmax(s, mask, axis=-1)
        return jnp.einsum('bghtk,bkgd->btghd', p.astype(qb.dtype), vb)

    o_win = lax.map(win_block, (q_win, jnp.arange(nw) * WIN_QUERY_BLOCK))
    o_win = jnp.moveaxis(o_win, 0, 1).reshape(B, S, G, HG, d)

    o = (gates[..., 0:1] * o_cmp + gates[..., 1:2] * o_slc + gates[..., 2:3] * o_win)
    return o.reshape(B, S, N_HEADS * d) @ w_o


def setup_inputs(seed: int = 0) -> dict:
    key = jax.random.key(seed)
    ks = iter(jax.random.split(key, 40))

    def nrm(shape, scale):
        return jax.random.normal(next(ks), shape, jnp.float32) * scale

    D = D_MODEL
    n_a = DEPTH // 2
    n_b = DEPTH - n_a
    kv_cols = 2 * N_BRANCHES * N_KV_GROUPS * HEAD_DIM
    qg_cols = N_HEADS * HEAD_DIM + N_HEADS * N_BRANCHES
    return {
        "x": nrm((BATCH, SEQ, D), 1.0),
        "c": nrm((BATCH, D), 1.0),
        "mod_w": nrm((DEPTH, 2, D, 3 * D), 0.5 * D ** -0.5),
        "mod_b": nrm((DEPTH, 2, 3 * D), 0.02),
        "norm_g": 1.0 + nrm((DEPTH, 2, D), 0.02),
        "mlp_w1": nrm((DEPTH, D, D_FF), D ** -0.5),
        "mlp_w2": nrm((DEPTH, D_FF, D), D_FF ** -0.5),
        "a_w_in": nrm((n_a, D, 2 * GMLP_WIDTH), D ** -0.5),
        "a_b_in": nrm((n_a, 2 * GMLP_WIDTH), 0.02),
        "a_ln_g": 1.0 + nrm((n_a, GMLP_WIDTH), 0.02),
        "a_ln_b": nrm((n_a, GMLP_WIDTH), 0.02),
        "a_w_s": nrm((n_a, GMLP_GROUPS, GMLP_CHUNK, GMLP_CHUNK), GMLP_CHUNK ** -0.5),
        "a_b_s": 1.0 + nrm((n_a, GMLP_GROUPS, GMLP_CHUNK), 0.02),
        "a_w_out": nrm((n_a, GMLP_WIDTH, D), GMLP_WIDTH ** -0.5),
        "a_b_out": nrm((n_a, D), 0.02),
        "kv_norm_g": 1.0 + nrm((D,), 0.02),
        "kv_mod_w": nrm((D, 2 * D), 0.5 * D ** -0.5),
        "kv_mod_b": nrm((2 * D,), 0.02),
        "w_kv": nrm((D, kv_cols), D ** -0.5),
        "cmp_pos_k": nrm((CMP_BLOCK, HEAD_DIM), 0.5),
        "cmp_w1_k": nrm((CMP_BLOCK * HEAD_DIM, CMP_HIDDEN), (CMP_BLOCK * HEAD_DIM) ** -0.5),
        "cmp_w2_k": nrm((CMP_HIDDEN, HEAD_DIM), CMP_HIDDEN ** -0.5),
        "cmp_pos_v": nrm((CMP_BLOCK, HEAD_DIM), 0.5),
        "cmp_w1_v": nrm((CMP_BLOCK * HEAD_DIM, CMP_HIDDEN), (CMP_BLOCK * HEAD_DIM) ** -0.5),
        "cmp_w2_v": nrm((CMP_HIDDEN, HEAD_DIM), CMP_HIDDEN ** -0.5),
        "b_w_qg": nrm((n_b, D, qg_cols), D ** -0.5),
        "b_w_o": nrm((n_b, N_HEADS * HEAD_DIM, D), (N_HEADS * HEAD_DIM) ** -0.5),
        "final_g": 1.0 + nrm((D,), 0.02),
    }


def reference(x, c, mod_w, mod_b, norm_g, mlp_w1, mlp_w2,
              a_w_in, a_b_in, a_ln_g, a_ln_b, a_w_s, a_b_s, a_w_out, a_b_out,
              kv_norm_g, kv_mod_w, kv_mod_b, w_kv,
              cmp_pos_k, cmp_w1_k, cmp_w2_k, cmp_pos_v, cmp_w1_v, cmp_w2_v,
              b_w_qg, b_w_o, final_g):
    n_a = a_w_in.shape[0]
    cond = jax.nn.silu(c)
    shared = None
    for layer in range(DEPTH):
        if layer == n_a:
            kv_shift, kv_scale = jnp.split(cond @ kv_mod_w + kv_mod_b, 2, axis=-1)
            h_kv = modulate(rms_norm(x, kv_norm_g), kv_shift, kv_scale)
            shared = nsa_shared_kv(h_kv, w_kv, cmp_pos_k, cmp_w1_k, cmp_w2_k,
                                   cmp_pos_v, cmp_w1_v, cmp_w2_v)
        mod = jnp.einsum('bd,sde->sbe', cond, mod_w[layer]) + mod_b[layer][:, None, :]

        shift, scale, gate = jnp.split(mod[0], 3, axis=-1)
        h = modulate(rms_norm(x, norm_g[layer, 0]), shift, scale)
        if layer < n_a:
            y = gmlp_mixer(h, a_w_in[layer], a_b_in[layer], a_ln_g[layer], a_ln_b[layer],
                           a_w_s[layer], a_b_s[layer], a_w_out[layer], a_b_out[layer])
        else:
            j = layer - n_a
            y = nsa_mixer(h, shared[0], shared[1], shared[2], shared[3], shared[4], shared[5],
                          b_w_qg[j], b_w_o[j])
        x = x + gate[:, None, :] * y

        shift, scale, gate = jnp.split(mod[1], 3, axis=-1)
        h = modulate(rms_norm(x, norm_g[layer, 1]), shift, scale)
        x = x + gate[:, None, :] * sq_relu_mlp(h, mlp_w1[layer], mlp_w2[layer])
    return rms_norm(x, final_g)
```

```python
import functools

import numpy as np
import jax
import jax.numpy as jnp
from jax import lax
from jax.experimental import pallas as pl
from jax.experimental.pallas import tpu as pltpu

BF16 = jnp.bfloat16
F32 = jnp.float32

LANES = 128
HEAD_DIM = 128
HALF_HEAD = HEAD_DIM // 2
N_BRANCHES = 3
CMP_BLOCK = 32
CMP_STRIDE = 16
SEL_BLOCK = 64
N_SELECT = 8
WINDOW = 512
ROPE_THETA = 10000.0
EPS = 1e-6
NEG_INF = -1e30
FORCE_BONUS = 1e6
ATTN_TILE = 128
VMEM_LIMIT = 56 * 1024 * 1024


def _dot(a, b):
    return jnp.dot(a, b, preferred_element_type=F32)


def _dot_nt(a, b):
    return lax.dot_general(a, b, (((1,), (1,)), ((), ())), preferred_element_type=F32)


def _tile(n, pref):
    return pref if n % pref == 0 else n


def _params(semantics):
    return pltpu.CompilerParams(dimension_semantics=semantics, vmem_limit_bytes=VMEM_LIMIT)


def _rms(x, g):
    return (x * lax.rsqrt(jnp.mean(x * x, axis=-1, keepdims=True) + EPS)) * g


def _norm_mod(x, g, shift, scale):
    return _rms(x, g) * (1.0 + scale) + shift


def _rope(a, cos, sin):
    return a * cos + pltpu.roll(a, HALF_HEAD, 1) * sin


def _cond_kernel(c_ref, w_ref, b_ref, o_ref):
    cond = jax.nn.silu(c_ref[...]).astype(BF16)
    o_ref[...] = _dot(cond, w_ref[...].astype(BF16)) + b_ref[...]


def _cond_matmul(c, w, b):
    n, d, e = w.shape
    bsz = c.shape[0]
    te = _tile(e, 1024)
    return pl.pallas_call(
        _cond_kernel,
        grid=(n, e // te),
        in_specs=[
            pl.BlockSpec((bsz, d), lambda s, j: (0, 0)),
            pl.BlockSpec((None, d, te), lambda s, j: (s, 0, j)),
            pl.BlockSpec((None, 1, te), lambda s, j: (s, 0, j)),
        ],
        out_specs=pl.BlockSpec((None, bsz, te), lambda s, j: (s, 0, j)),
        out_shape=jax.ShapeDtypeStruct((n, bsz, e), F32),
        compiler_params=_params(("parallel", "parallel")),
        name="cond_matmul",
    )(c, w, b.reshape(n, 1, e))


def _gmlp_in_kernel(x_ref, g_ref, sh_ref, sc_ref, w_ref, b_ref, u_ref, v_ref, h_scr, *, n_u):
    j = pl.program_id(2)

    @pl.when(j == 0)
    def _():
        h_scr[...] = _norm_mod(x_ref[...], g_ref[...], sh_ref[...], sc_ref[...]).astype(BF16)

    z = jax.nn.gelu(_dot(h_scr[...], w_ref[...]) + b_ref[...])

    @pl.when(j < n_u)
    def _():
        u_ref[...] = z

    @pl.when(j >= n_u)
    def _():
        v_ref[...] = z


def _gmlp_in(x, g, shift, scale, w, b):
    bsz, s, d = x.shape
    n = w.shape[1]
    width = n // 2
    tm = _tile(s, 1024)
    tn = _tile(width, 512)
    n_u = width // tn
    row = lambda bi, i, j: (bi, 0, 0)
    return pl.pallas_call(
        functools.partial(_gmlp_in_kernel, n_u=n_u),
        grid=(bsz, s // tm, n // tn),
        in_specs=[
            pl.BlockSpec((None, tm, d), lambda bi, i, j: (bi, i, 0)),
            pl.BlockSpec((1, d), lambda bi, i, j: (0, 0)),
            pl.BlockSpec((None, 1, d), row),
            pl.BlockSpec((None, 1, d), row),
            pl.BlockSpec((d, tn), lambda bi, i, j: (0, j)),
            pl.BlockSpec((1, tn), lambda bi, i, j: (0, j)),
        ],
        out_specs=[
            pl.BlockSpec((None, tm, tn), lambda bi, i, j: (bi, i, jnp.minimum(j, n_u - 1))),
            pl.BlockSpec((None, tm, tn), lambda bi, i, j: (bi, i, jnp.maximum(j - n_u, 0))),
        ],
        out_shape=[jax.ShapeDtypeStruct((bsz, s, width), F32)] * 2,
        scratch_shapes=[pltpu.VMEM((tm, d), BF16)],
        compiler_params=_params(("parallel", "parallel", "arbitrary")),
        name="gmlp_in",
    )(x, g.reshape(1, d), shift, scale, w, b.reshape(1, n))


def _gmlp_gate_kernel(u_ref, v_ref, lg_ref, lb_ref, ws_ref, bs_ref, o_ref, *, groups, gdim):
    v = v_ref[...]
    mu = jnp.mean(v, axis=-1, keepdims=True)
    dv = v - mu
    var = jnp.mean(dv * dv, axis=-1, keepdims=True)
    vn = ((dv * lax.rsqrt(var + EPS)) * lg_ref[...] + lb_ref[...]).astype(BF16)
    chunk = v.shape[0]
    causal = (lax.broadcasted_iota(jnp.int32, (chunk, chunk), 1)
              <= lax.broadcasted_iota(jnp.int32, (chunk, chunk), 0))
    for gi in range(groups):
        cols = slice(gi * gdim, (gi + 1) * gdim)
        w = jnp.where(causal, ws_ref[gi], 0.0).astype(BF16)
        mixed = _dot(w, vn[:, cols]) + bs_ref[:, gi:gi + 1]
        o_ref[:, cols] = (u_ref[:, cols] * mixed).astype(BF16)


def _gmlp_gate(u, v, ln_g, ln_b, w_s, b_s):
    bsz, s, width = u.shape
    groups, chunk, _ = w_s.shape
    gdim = width // groups
    blk = pl.BlockSpec((None, chunk, width), lambda bi, i: (bi, i, 0))
    vec = pl.BlockSpec((1, width), lambda bi, i: (0, 0))
    return pl.pallas_call(
        functools.partial(_gmlp_gate_kernel, groups=groups, gdim=gdim),
        grid=(bsz, s // chunk),
        in_specs=[blk, blk, vec, vec,
                  pl.BlockSpec((groups, chunk, chunk), lambda bi, i: (0, 0, 0)),
                  pl.BlockSpec((chunk, groups), lambda bi, i: (0, 0))],
        out_specs=blk,
        out_shape=jax.ShapeDtypeStruct((bsz, s, width), BF16),
        compiler_params=_params(("parallel", "parallel")),
        name="gmlp_gate",
    )(u, v, ln_g.reshape(1, width), ln_b.reshape(1, width), w_s, b_s.T)


def _mm_res_kernel(a_ref, w_ref, b_ref, x_ref, gate_ref, o_ref):
    y = _dot(a_ref[...], w_ref[...]) + b_ref[...]
    o_ref[...] = x_ref[...] + gate_ref[...] * y


def _mm_residual(a, w, b, x, gate):
    bsz, s, k = a.shape
    d = w.shape[1]
    tm = _tile(s, 1024)
    tn = _tile(d, 512)
    return pl.pallas_call(
        _mm_res_kernel,
        grid=(bsz, s // tm, d // tn),
        in_specs=[
            pl.BlockSpec((None, tm, k), lambda bi, i, j: (bi, i, 0)),
            pl.BlockSpec((k, tn), lambda bi, i, j: (0, j)),
            pl.BlockSpec((1, tn), lambda bi, i, j: (0, j)),
            pl.BlockSpec((None, tm, tn), lambda bi, i, j: (bi, i, j)),
            pl.BlockSpec((None, 1, tn), lambda bi, i, j: (bi, 0, j)),
        ],
        out_specs=pl.BlockSpec((None, tm, tn), lambda bi, i, j: (bi, i, j)),
        out_shape=jax.ShapeDtypeStruct((bsz, s, d), F32),
        compiler_params=_params(("parallel", "parallel", "parallel")),
        name="matmul_residual",
    )(a, w, b.reshape(1, d), x, gate)


def _mlp_kernel(x_ref, g_ref, sh_ref, sc_ref, gate_ref, w1_ref, w2_ref, fg_ref, o_ref,
                h_scr, acc_scr, *, final_norm):
    f = pl.program_id(2)

    @pl.when(f == 0)
    def _():
        h_scr[...] = _norm_mod(x_ref[...], g_ref[...], sh_ref[...], sc_ref[...]).astype(BF16)
        acc_scr[...] = jnp.zeros_like(acc_scr)

    a = jnp.square(jnp.maximum(_dot(h_scr[...], w1_ref[...]), 0.0)).astype(BF16)
    acc_scr[...] += _dot(a, w2_ref[...])

    @pl.when(f == pl.num_programs(2) - 1)
    def _():
        xn = x_ref[...] + gate_ref[...] * acc_scr[...]
        if final_norm:
            xn = _rms(xn, fg_ref[...])
        o_ref[...] = xn


def _mlp(x, g, shift, scale, gate, w1, w2, final_g, final_norm):
    bsz, s, d = x.shape
    dff = w1.shape[1]
    tm = _tile(s, 512)
    tf = _tile(dff, 1024)
    row = lambda bi, i, f: (bi, 0, 0)
    vec = pl.BlockSpec((1, d), lambda bi, i, f: (0, 0))
    xblk = pl.BlockSpec((None, tm, d), lambda bi, i, f: (bi, i, 0))
    return pl.pallas_call(
        functools.partial(_mlp_kernel, final_norm=final_norm),
        grid=(bsz, s // tm, dff // tf),
        in_specs=[
            xblk, vec,
            pl.BlockSpec((None, 1, d), row), pl.BlockSpec((None, 1, d), row),
            pl.BlockSpec((None, 1, d), row),
            pl.BlockSpec((d, tf), lambda bi, i, f: (0, f)),
            pl.BlockSpec((tf, d), lambda bi, i, f: (f, 0)),
            vec,
        ],
        out_specs=xblk,
        out_shape=jax.ShapeDtypeStruct((bsz, s, d), F32),
        scratch_shapes=[pltpu.VMEM((tm, d), BF16), pltpu.VMEM((tm, d), F32)],
        compiler_params=_params(("parallel", "parallel", "arbitrary")),
        name="mlp",
    )(x, g.reshape(1, d), shift, scale, gate, w1, w2, final_g.reshape(1, d))


def _kv_kernel(x_ref, g_ref, sh_ref, sc_ref, w_ref, cos_ref, sin_ref, cmp_ref, rest_ref, h_scr,
               *, groups):
    j = pl.program_id(2)

    @pl.when(j == 0)
    def _():
        h_scr[...] = _norm_mod(x_ref[...], g_ref[...], sh_ref[...], sc_ref[...]).astype(BF16)

    acc = _dot(h_scr[...], w_ref[...])

    def emit(dst, rotate):
        for gi in range(groups):
            a = acc[:, gi * HEAD_DIM:(gi + 1) * HEAD_DIM]
            if rotate:
                a = _rope(a, cos_ref[...], sin_ref[...])
            dst[gi] = a.astype(dst.dtype)

    @pl.when(j == 0)
    def _():
        emit(cmp_ref, True)

    @pl.when(j == 1)
    def _():
        emit(cmp_ref, False)

    @pl.when((j >= 2) & (j % 2 == 0))
    def _():
        emit(rest_ref, True)

    @pl.when((j >= 2) & (j % 2 == 1))
    def _():
        emit(rest_ref, False)


def _kv_proj(x, g, shift, scale, w, cos, sin, groups):
    bsz, s, d = x.shape
    tm = _tile(s, 1024)
    tn = groups * HEAD_DIM
    row = lambda bi, i, j: (bi, 0, 0)
    tab = pl.BlockSpec((tm, HEAD_DIM), lambda bi, i, j: (i, 0))
    return pl.pallas_call(
        functools.partial(_kv_kernel, groups=groups),
        grid=(bsz, s // tm, 2 * N_BRANCHES),
        in_specs=[
            pl.BlockSpec((None, tm, d), lambda bi, i, j: (bi, i, 0)),
            pl.BlockSpec((1, d), lambda bi, i, j: (0, 0)),
            pl.BlockSpec((None, 1, d), row), pl.BlockSpec((None, 1, d), row),
            pl.BlockSpec((d, tn), lambda bi, i, j: (0, j)),
            tab, tab,
        ],
        out_specs=[
            pl.BlockSpec((None, None, groups, tm, HEAD_DIM),
                         lambda bi, i, j: (bi, jnp.minimum(j, 1), 0, i, 0)),
            pl.BlockSpec((None, None, groups, tm, HEAD_DIM),
                         lambda bi, i, j: (bi, jnp.maximum(j - 2, 0), 0, i, 0)),
        ],
        out_shape=[jax.ShapeDtypeStruct((bsz, 2, groups, s, HEAD_DIM), F32),
                   jax.ShapeDtypeStruct((bsz, 4, groups, s, HEAD_DIM), BF16)],
        scratch_shapes=[pltpu.VMEM((tm, d), BF16)],
        compiler_params=_params(("parallel", "parallel", "arbitrary")),
        name="kv_proj",
    )(x, g.reshape(1, d), shift, scale, w, cos, sin)


def _qg_kernel(x_ref, g_ref, sh_ref, sc_ref, w_ref, cos_ref, sin_ref, q_ref, gate_ref, h_scr,
               *, heads_per_tile, n_q):
    j = pl.program_id(2)

    @pl.when(j == 0)
    def _():
        h_scr[...] = _norm_mod(x_ref[...], g_ref[...], sh_ref[...], sc_ref[...]).astype(BF16)

    acc = _dot(h_scr[...], w_ref[...])

    @pl.when(j < n_q)
    def _():
        for hi in range(heads_per_tile):
            cols = slice(hi * HEAD_DIM, (hi + 1) * HEAD_DIM)
            a = _rope(acc[:, cols], cos_ref[...], sin_ref[...]) * (HEAD_DIM ** -0.5)
            q_ref[:, cols] = a.astype(BF16)

    @pl.when(j >= n_q)
    def _():
        gate_ref[...] = jax.nn.sigmoid(acc)


def _qg_proj(x, g, shift, scale, w_q, w_gate, cos, sin, tn):
    bsz, s, d = x.shape
    nq_cols, ng_cols = w_q.shape[1], w_gate.shape[1]
    n_q, n_g = nq_cols // tn, ng_cols // tn
    tm = _tile(s, 1024)
    row = lambda bi, i, j: (bi, 0, 0)
    tab = pl.BlockSpec((tm, HEAD_DIM), lambda bi, i, j: (i, 0))
    w = jnp.concatenate([w_q, w_gate], axis=1)
    return pl.pallas_call(
        functools.partial(_qg_kernel, heads_per_tile=tn // HEAD_DIM, n_q=n_q),
        grid=(bsz, s // tm, n_q + n_g),
        in_specs=[
            pl.BlockSpec((None, tm, d), lambda bi, i, j: (bi, i, 0)),
            pl.BlockSpec((1, d), lambda bi, i, j: (0, 0)),
            pl.BlockSpec((None, 1, d), row), pl.BlockSpec((None, 1, d), row),
            pl.BlockSpec((d, tn), lambda bi, i, j: (0, j)),
            tab, tab,
        ],
        out_specs=[
            pl.BlockSpec((None, tm, tn), lambda bi, i, j: (bi, i, jnp.minimum(j, n_q - 1))),
            pl.BlockSpec((None, tm, tn), lambda bi, i, j: (bi, i, jnp.maximum(j - n_q, 0))),
        ],
        out_shape=[jax.ShapeDtypeStruct((bsz, s, nq_cols), BF16),
                   jax.ShapeDtypeStruct((bsz, s, ng_cols), F32)],
        scratch_shapes=[pltpu.VMEM((tm, d), BF16)],
        compiler_params=_params(("parallel", "parallel", "arbitrary")),
        name="qg_proj",
    )(x, g.reshape(1, d), shift, scale, w, cos, sin)


def _compress_kernel(h_ref, pos_ref, w1_ref, w2_ref, o_ref):
    hv = h_ref[...]
    half = hv.shape[1]
    lo = (hv + pos_ref[0:1, :]).astype(BF16)
    hi = (hv + pos_ref[1:2, :]).astype(BF16)
    first = _dot(lo, w1_ref[0:half, :])
    second = _dot(hi, w1_ref[half:2 * half, :])
    n_rows = hv.shape[0]
    pre = first + pltpu.roll(second, n_rows - 1, 0)
    o_ref[...] = _dot(jax.nn.gelu(pre).astype(BF16), w2_ref[...]).astype(BF16)


def _compress(kv_cmp, pos, w1, w2):
    bsz, _, groups, s, _ = kv_cmp.shape
    n_rows = s // CMP_STRIDE
    flat = CMP_STRIDE * HEAD_DIM
    hidden = w1.shape[2]
    h = kv_cmp.reshape(bsz, 2, groups, n_rows, flat)
    return pl.pallas_call(
        _compress_kernel,
        grid=(bsz, 2, groups),
        in_specs=[
            pl.BlockSpec((None, None, None, n_rows, flat), lambda bi, t, gi: (bi, t, gi, 0, 0)),
            pl.BlockSpec((None, 2, flat), lambda bi, t, gi: (t, 0, 0)),
            pl.BlockSpec((None, 2 * flat, hidden), lambda bi, t, gi: (t, 0, 0)),
            pl.BlockSpec((None, hidden, HEAD_DIM), lambda bi, t, gi: (t, 0, 0)),
        ],
        out_specs=pl.BlockSpec((None, None, None, n_rows, HEAD_DIM),
                               lambda bi, t, gi: (bi, t, gi, 0, 0)),
        out_shape=jax.ShapeDtypeStruct((bsz, 2, groups, n_rows, HEAD_DIM), BF16),
        compiler_params=_params(("parallel", "parallel", "parallel")),
        name="compress",
    )(h, pos, w1, w2)


def _masked_softmax(s, mask):
    s = jnp.where(mask, s, NEG_INF)
    m = jnp.max(s, axis=-1, keepdims=True)
    e = jnp.where(mask, jnp.exp(s - m), 0.0)
    den = jnp.sum(e, axis=-1, keepdims=True)
    return e / jnp.maximum(den, 1e-30)


def _attn_kernel(q_ref, gate_ref, kc_ref, vc_ref, ks_ref, vs_ref, kw_ref, vw_ref, ov_ref, ex_ref,
                 o_ref, selm_scr, m_scr, l_scr, acc_scr, *, hg, n_cmp, n_slc, n_sel):
    tq = tk = ATTN_TILE
    i = pl.program_id(2)
    t0 = i * tq
    q = q_ref[...]
    q4 = jnp.concatenate([q[:, h * HEAD_DIM:(h + 1) * HEAD_DIM] for h in range(hg)], axis=0)
    tpos = t0 + lax.broadcasted_iota(jnp.int32, (tq, 1), 0)
    tpos4 = jnp.concatenate([tpos] * hg, axis=0)
    lane = lax.broadcasted_iota(jnp.int32, (1, LANES), 1)

    s_cmp = _dot_nt(q4, kc_ref[...])
    valid = (lane * CMP_STRIDE + (CMP_BLOCK - 1) <= tpos4) & (lane < n_cmp)
    p_cmp = _masked_softmax(s_cmp, valid).astype(BF16)
    o_cmp = _dot(p_cmp, vc_ref[...])

    p_heads = jnp.concatenate([p_cmp[h * tq:(h + 1) * tq] for h in range(hg)], axis=1)
    imp = _dot(p_heads, ov_ref[...])
    cur = tpos // SEL_BLOCK
    forced = (lane == 0) | (lane == cur) | (lane == cur - 1)
    imp = jnp.where((lane * SEL_BLOCK <= tpos) & (lane < n_slc),
                    imp + jnp.where(forced, FORCE_BONUS, 0.0), NEG_INF)
    rank = jnp.zeros((tq, LANES), jnp.int32)
    for c in range(n_slc):
        col = imp[:, c:c + 1]
        beats = (col > imp) | ((col == imp) & (c < lane))
        rank = rank + jnp.where(beats, 1, 0)
    sel = jnp.where(rank < n_sel, 1.0, 0.0).astype(BF16)
    selm_scr[...] = _dot(sel, ex_ref[...])

    def attend(k_ref, v_ref, first_tile, mask_fn):
        m_scr[...] = jnp.full_like(m_scr, NEG_INF)
        l_scr[...] = jnp.zeros_like(l_scr)
        acc_scr[...] = jnp.zeros_like(acc_scr)

        def body(kt, carry):
            k0 = pl.multiple_of(kt * tk, tk)
            kpos = k0 + lax.broadcasted_iota(jnp.int32, (1, tk), 1)
            mask = mask_fn(k0, kpos)
            s = jnp.where(mask, _dot_nt(q4, k_ref[pl.ds(k0, tk), :]), NEG_INF)
            m_old = m_scr[...]
            m_new = jnp.maximum(m_old, jnp.max(s, axis=-1, keepdims=True))
            alpha = jnp.exp(m_old - m_new)
            e = jnp.where(mask, jnp.exp(s - m_new), 0.0)
            l_scr[...] = alpha * l_scr[...] + jnp.sum(e, axis=-1, keepdims=True)
            acc_scr[...] = alpha * acc_scr[...] + _dot(e.astype(BF16), v_ref[pl.ds(k0, tk), :])
            m_scr[...] = m_new
            return carry

        lax.fori_loop(first_tile, i + 1, body, 0)
        return acc_scr[...] / jnp.maximum(l_scr[...], 1e-30)

    def slc_mask(k0, kpos):
        picked = selm_scr[:, pl.ds(k0, tk)]
        return (jnp.concatenate([picked] * hg, axis=0) > 0.5) & (kpos <= tpos4)

    def win_mask(k0, kpos):
        return (kpos <= tpos4) & (kpos > tpos4 - WINDOW)

    o_slc = attend(ks_ref, vs_ref, 0, slc_mask)
    o_win = attend(kw_ref, vw_ref, jnp.maximum(i - WINDOW // tk, 0), win_mask)

    gates = gate_ref[...]
    for h in range(hg):
        rows = slice(h * tq, (h + 1) * tq)
        c0 = h * N_BRANCHES
        o = (gates[:, c0:c0 + 1] * o_cmp[rows] + gates[:, c0 + 1:c0 + 2] * o_slc[rows]
             + gates[:, c0 + 2:c0 + 3] * o_win[rows])
        o_ref[:, h * HEAD_DIM:(h + 1) * HEAD_DIM] = o.astype(BF16)


def _attention(q, gates, kv_cmp, kv_rest, hg):
    bsz, s, _ = q.shape
    groups = kv_rest.shape[2]
    n_rows = kv_cmp.shape[3]
    n_cmp = n_rows - 1
    n_slc = s // SEL_BLOCK
    n_sel = min(N_SELECT, n_slc)
    assert n_rows == LANES and n_slc <= LANES and s % ATTN_TILE == 0
    tq = ATTN_TILE

    ci = np.arange(LANES)[:, None]
    sj = np.arange(LANES)[None, :]
    overlap = ((ci * CMP_STRIDE <= sj * SEL_BLOCK + SEL_BLOCK - 1)
               & (ci * CMP_STRIDE + CMP_BLOCK - 1 >= sj * SEL_BLOCK) & (ci < n_cmp) & (sj < n_slc))
    overlap = jnp.asarray(np.tile(overlap, (hg, 1)), BF16)
    expand = jnp.asarray(np.arange(s)[None, :] // SEL_BLOCK == np.arange(LANES)[:, None], BF16)

    def kv_spec(which):
        return pl.BlockSpec((None, None, None, s, HEAD_DIM), lambda bi, gi, i: (bi, which, gi, 0, 0))

    def cmp_spec(which):
        return pl.BlockSpec((None, None, None, n_rows, HEAD_DIM), lambda bi, gi, i: (bi, which, gi, 0, 0))

    qblk = pl.BlockSpec((None, tq, hg * HEAD_DIM), lambda bi, gi, i: (bi, i, gi))
    return pl.pallas_call(
        functools.partial(_attn_kernel, hg=hg, n_cmp=n_cmp, n_slc=n_slc, n_sel=n_sel),
        grid=(bsz, groups, s // tq),
        in_specs=[
            qblk,
            pl.BlockSpec((None, tq, LANES), lambda bi, gi, i: (bi, i, gi)),
            cmp_spec(0), cmp_spec(1),
            kv_spec(0), kv_spec(1), kv_spec(2), kv_spec(3),
            pl.BlockSpec((hg * LANES, LANES), lambda bi, gi, i: (0, 0)),
            pl.BlockSpec((LANES, s), lambda bi, gi, i: (0, 0)),
        ],
        out_specs=qblk,
        out_shape=jax.ShapeDtypeStruct(q.shape, BF16),
        scratch_shapes=[pltpu.VMEM((tq, s), F32),
                        pltpu.VMEM((hg * tq, 1), F32), pltpu.VMEM((hg * tq, 1), F32),
                        pltpu.VMEM((hg * tq, HEAD_DIM), F32)],
        compiler_params=_params(("parallel", "parallel", "arbitrary")),
        name="nsa_attention",
    )(q, gates, kv_cmp, kv_cmp, kv_rest, kv_rest, kv_rest, kv_rest, overlap, expand)


def _rope_tables(s):
    freqs = ROPE_THETA ** (-jnp.arange(HALF_HEAD, dtype=F32) / HALF_HEAD)
    ang = jnp.arange(s).astype(F32)[:, None] * freqs[None, :]
    cos, sin = jnp.cos(ang), jnp.sin(ang)
    return jnp.concatenate([cos, cos], axis=1), jnp.concatenate([-sin, sin], axis=1)


def kernel(x, c, mod_w, mod_b, norm_g, mlp_w1, mlp_w2, a_w_in, a_b_in, a_ln_g, a_ln_b, a_w_s, a_b_s, a_w_out, a_b_out, kv_norm_g, kv_mod_w, kv_mod_b, w_kv, cmp_pos_k, cmp_w1_k, cmp_w2_k, cmp_pos_v, cmp_w1_v, cmp_w2_v, b_w_qg, b_w_o, final_g):
    bsz, s, d = x.shape
    depth = mod_w.shape[0]
    n_a = a_w_in.shape[0]
    groups = w_kv.shape[1] // (2 * N_BRANCHES * HEAD_DIM)
    heads = b_w_o.shape[1] // HEAD_DIM
    hg = heads // groups
    assert hg * N_BRANCHES <= LANES

    mod = _cond_matmul(c, mod_w.reshape(depth * 2, d, 3 * d), mod_b.reshape(depth * 2, 3 * d))
    mod = mod.reshape(depth, 2, bsz, 3, 1, d)
    kv_mod = _cond_matmul(c, kv_mod_w[None], kv_mod_b[None]).reshape(bsz, 2, 1, d)
    cos, sin = _rope_tables(s)

    for layer in range(depth):
        shift, scale, gate = (mod[layer, 0, :, t] for t in range(3))
        if layer < n_a:
            u, v = _gmlp_in(x, norm_g[layer, 0], shift, scale,
                            a_w_in[layer].astype(BF16), a_b_in[layer])
            mixed = _gmlp_gate(u, v, a_ln_g[layer], a_ln_b[layer], a_w_s[layer], a_b_s[layer])
            x = _mm_residual(mixed, a_w_out[layer].astype(BF16), a_b_out[layer], x, gate)
        else:
            j = layer - n_a
            if layer == n_a:
                kv_cmp, kv_rest = _kv_proj(x, kv_norm_g, kv_mod[:, 0], kv_mod[:, 1],
                                           w_kv.astype(BF16), cos, sin, groups)
                pos = jnp.stack([cmp_pos_k, cmp_pos_v]).reshape(2, 2, CMP_STRIDE * HEAD_DIM)
                kv_cmp = _compress(kv_cmp, pos,
                                   jnp.stack([cmp_w1_k, cmp_w1_v]).astype(BF16),
                                   jnp.stack([cmp_w2_k, cmp_w2_v]).astype(BF16))
            w_qg = b_w_qg[j]
            w_q = w_qg[:, :heads * HEAD_DIM]
            w_gate = w_qg[:, heads * HEAD_DIM:].reshape(d, groups, hg * N_BRANCHES)
            w_gate = jnp.pad(w_gate, ((0, 0), (0, 0), (0, LANES - hg * N_BRANCHES)))
            w_gate = w_gate.reshape(d, groups * LANES)
            tn = hg * HEAD_DIM
            pad = (-w_gate.shape[1]) % tn
            w_gate = jnp.pad(w_gate, ((0, 0), (0, pad)))
            q, gates = _qg_proj(x, norm_g[layer, 0], shift, scale,
                                w_q.astype(BF16), w_gate.astype(BF16), cos, sin, tn)
            o = _attention(q, gates, kv_cmp, kv_rest, hg)
            x = _mm_residual(o, b_w_o[j].astype(BF16), jnp.zeros((d,), F32), x, gate)

        shift, scale, gate = (mod[layer, 1, :, t] for t in range(3))
        x = _mlp(x, norm_g[layer, 1], shift, scale, gate,
                 mlp_w1[layer].astype(BF16), mlp_w2[layer].astype(BF16),
                 final_g, final_norm=(layer == depth - 1))
    return x
```

```python
import functools

import numpy as np
import jax
import jax.numpy as jnp
from jax import lax
from jax.experimental import pallas as pl
from jax.experimental.pallas import tpu as pltpu

BF16 = jnp.bfloat16
F32 = jnp.float32

LANES = 128
HEAD_DIM = 128
HALF_HEAD = HEAD_DIM // 2
N_BRANCHES = 3
CMP_BLOCK = 32
CMP_STRIDE = 16
SEL_BLOCK = 64
N_SELECT = 8
WINDOW = 512
ROPE_THETA = 10000.0
EPS = 1e-6
NEG_INF = -1e30
FORCE_BONUS = 1e6
ATTN_TILE = 128
SLC_TILE = 256
VMEM_LIMIT = 56 * 1024 * 1024


def _dot(a, b):
    return jnp.dot(a, b, preferred_element_type=F32)


def _dot_nt(a, b):
    return lax.dot_general(a, b, (((1,), (1,)), ((), ())), preferred_element_type=F32)


def _tile(n, pref):
    return pref if n % pref == 0 else n


def _params(semantics):
    return pltpu.CompilerParams(dimension_semantics=semantics, vmem_limit_bytes=VMEM_LIMIT)


def _rms(x, g):
    return (x * lax.rsqrt(jnp.mean(x * x, axis=-1, keepdims=True) + EPS)) * g


def _norm_mod(x, g, shift, scale):
    return _rms(x, g) * (1.0 + scale) + shift


def _rope(a, cos, sin):
    return a * cos + pltpu.roll(a, HALF_HEAD, 1) * sin


def _cond_kernel(c_ref, w_ref, b_ref, o_ref):
    cond = jax.nn.silu(c_ref[...]).astype(BF16)
    o_ref[...] = _dot(cond, w_ref[...].astype(BF16)) + b_ref[...]


def _cond_matmul(c, w, b):
    n, d, e = w.shape
    bsz = c.shape[0]
    te = _tile(e, 1024)
    return pl.pallas_call(
        _cond_kernel,
        grid=(n, e // te),
        in_specs=[
            pl.BlockSpec((bsz, d), lambda s, j: (0, 0)),
            pl.BlockSpec((None, d, te), lambda s, j: (s, 0, j)),
            pl.BlockSpec((None, 1, te), lambda s, j: (s, 0, j)),
        ],
        out_specs=pl.BlockSpec((None, bsz, te), lambda s, j: (s, 0, j)),
        out_shape=jax.ShapeDtypeStruct((n, bsz, e), F32),
        compiler_params=_params(("parallel", "parallel")),
        name="cond_matmul",
    )(c, w, b.reshape(n, 1, e))


def _gmlp_in_kernel(x_ref, g_ref, sh_ref, sc_ref, w_ref, b_ref, u_ref, v_ref, h_scr, *, n_u):
    j = pl.program_id(2)

    @pl.when(j == 0)
    def _():
        h_scr[...] = _norm_mod(x_ref[...], g_ref[...], sh_ref[...], sc_ref[...]).astype(BF16)

    z = jax.nn.gelu(_dot(h_scr[...], w_ref[...]) + b_ref[...])

    @pl.when(j < n_u)
    def _():
        u_ref[...] = z

    @pl.when(j >= n_u)
    def _():
        v_ref[...] = z


def _gmlp_in(x, g, shift, scale, w, b):
    bsz, s, d = x.shape
    n = w.shape[1]
    width = n // 2
    tm = _tile(s, 1024)
    tn = _tile(width, 512)
    n_u = width // tn
    row = lambda bi, i, j: (bi, 0, 0)
    return pl.pallas_call(
        functools.partial(_gmlp_in_kernel, n_u=n_u),
        grid=(bsz, s // tm, n // tn),
        in_specs=[
            pl.BlockSpec((None, tm, d), lambda bi, i, j: (bi, i, 0)),
            pl.BlockSpec((1, d), lambda bi, i, j: (0, 0)),
            pl.BlockSpec((None, 1, d), row),
            pl.BlockSpec((None, 1, d), row),
            pl.BlockSpec((d, tn), lambda bi, i, j: (0, j)),
            pl.BlockSpec((1, tn), lambda bi, i, j: (0, j)),
        ],
        out_specs=[
            pl.BlockSpec((None, tm, tn), lambda bi, i, j: (bi, i, jnp.minimum(j, n_u - 1))),
            pl.BlockSpec((None, tm, tn), lambda bi, i, j: (bi, i, jnp.maximum(j - n_u, 0))),
        ],
        out_shape=[jax.ShapeDtypeStruct((bsz, s, width), F32)] * 2,
        scratch_shapes=[pltpu.VMEM((tm, d), BF16)],
        compiler_params=_params(("parallel", "parallel", "arbitrary")),
        name="gmlp_in",
    )(x, g.reshape(1, d), shift, scale, w, b.reshape(1, n))


def _gmlp_gate_kernel(u_ref, v_ref, lg_ref, lb_ref, ws_ref, bs_ref, o_ref, *, groups, gdim):
    v = v_ref[...]
    mu = jnp.mean(v, axis=-1, keepdims=True)
    dv = v - mu
    var = jnp.mean(dv * dv, axis=-1, keepdims=True)
    vn = ((dv * lax.rsqrt(var + EPS)) * lg_ref[...] + lb_ref[...]).astype(BF16)
    chunk = v.shape[0]
    causal = (lax.broadcasted_iota(jnp.int32, (chunk, chunk), 1)
              <= lax.broadcasted_iota(jnp.int32, (chunk, chunk), 0))
    for gi in range(groups):
        cols = slice(gi * gdim, (gi + 1) * gdim)
        w = jnp.where(causal, ws_ref[gi], 0.0).astype(BF16)
        mixed = _dot(w, vn[:, cols]) + bs_ref[:, gi:gi + 1]
        o_ref[:, cols] = (u_ref[:, cols] * mixed).astype(BF16)


def _gmlp_gate(u, v, ln_g, ln_b, w_s, b_s):
    bsz, s, width = u.shape
    groups, chunk, _ = w_s.shape
    gdim = width // groups
    blk = pl.BlockSpec((None, chunk, width), lambda bi, i: (bi, i, 0))
    vec = pl.BlockSpec((1, width), lambda bi, i: (0, 0))
    return pl.pallas_call(
        functools.partial(_gmlp_gate_kernel, groups=groups, gdim=gdim),
        grid=(bsz, s // chunk),
        in_specs=[blk, blk, vec, vec,
                  pl.BlockSpec((groups, chunk, chunk), lambda bi, i: (0, 0, 0)),
                  pl.BlockSpec((chunk, groups), lambda bi, i: (0, 0))],
        out_specs=blk,
        out_shape=jax.ShapeDtypeStruct((bsz, s, width), BF16),
        compiler_params=_params(("parallel", "parallel")),
        name="gmlp_gate",
    )(u, v, ln_g.reshape(1, width), ln_b.reshape(1, width), w_s, b_s.T)


def _mm_res_kernel(a_ref, w_ref, b_ref, x_ref, gate_ref, o_ref):
    y = _dot(a_ref[...], w_ref[...]) + b_ref[...]
    o_ref[...] = x_ref[...] + gate_ref[...] * y


def _mm_residual(a, w, b, x, gate):
    bsz, s, k = a.shape
    d = w.shape[1]
    tm = _tile(s, 1024)
    tn = _tile(d, 512)
    return pl.pallas_call(
        _mm_res_kernel,
        grid=(bsz, s // tm, d // tn),
        in_specs=[
            pl.BlockSpec((None, tm, k), lambda bi, i, j: (bi, i, 0)),
            pl.BlockSpec((k, tn), lambda bi, i, j: (0, j)),
            pl.BlockSpec((1, tn), lambda bi, i, j: (0, j)),
            pl.BlockSpec((None, tm, tn), lambda bi, i, j: (bi, i, j)),
            pl.BlockSpec((None, 1, tn), lambda bi, i, j: (bi, 0, j)),
        ],
        out_specs=pl.BlockSpec((None, tm, tn), lambda bi, i, j: (bi, i, j)),
        out_shape=jax.ShapeDtypeStruct((bsz, s, d), F32),
        compiler_params=_params(("parallel", "parallel", "parallel")),
        name="matmul_residual",
    )(a, w, b.reshape(1, d), x, gate)


def _mlp_kernel(x_ref, g_ref, sh_ref, sc_ref, gate_ref, w1_ref, w2_ref, fg_ref, o_ref,
                h_scr, acc_scr, *, final_norm):
    f = pl.program_id(2)

    @pl.when(f == 0)
    def _():
        h_scr[...] = _norm_mod(x_ref[...], g_ref[...], sh_ref[...], sc_ref[...]).astype(BF16)
        acc_scr[...] = jnp.zeros_like(acc_scr)

    a = jnp.square(jnp.maximum(_dot(h_scr[...], w1_ref[...]), 0.0)).astype(BF16)
    acc_scr[...] += _dot(a, w2_ref[...])

    @pl.when(f == pl.num_programs(2) - 1)
    def _():
        xn = x_ref[...] + gate_ref[...] * acc_scr[...]
        if final_norm:
            xn = _rms(xn, fg_ref[...])
        o_ref[...] = xn


def _mlp(x, g, shift, scale, gate, w1, w2, final_g, final_norm):
    bsz, s, d = x.shape
    dff = w1.shape[1]
    tm = _tile(s, 512)
    tf = _tile(dff, 1024)
    row = lambda bi, i, f: (bi, 0, 0)
    vec = pl.BlockSpec((1, d), lambda bi, i, f: (0, 0))
    xblk = pl.BlockSpec((None, tm, d), lambda bi, i, f: (bi, i, 0))
    return pl.pallas_call(
        functools.partial(_mlp_kernel, final_norm=final_norm),
        grid=(bsz, s // tm, dff // tf),
        in_specs=[
            xblk, vec,
            pl.BlockSpec((None, 1, d), row), pl.BlockSpec((None, 1, d), row),
            pl.BlockSpec((None, 1, d), row),
            pl.BlockSpec((d, tf), lambda bi, i, f: (0, f)),
            pl.BlockSpec((tf, d), lambda bi, i, f: (f, 0)),
            vec,
        ],
        out_specs=xblk,
        out_shape=jax.ShapeDtypeStruct((bsz, s, d), F32),
        scratch_shapes=[pltpu.VMEM((tm, d), BF16), pltpu.VMEM((tm, d), F32)],
        compiler_params=_params(("parallel", "parallel", "arbitrary")),
        name="mlp",
    )(x, g.reshape(1, d), shift, scale, gate, w1, w2, final_g.reshape(1, d))


def _kv_kernel(x_ref, g_ref, sh_ref, sc_ref, w_ref, cos_ref, sin_ref, cmp_ref, rest_ref, h_scr,
               *, groups):
    j = pl.program_id(2)

    @pl.when(j == 0)
    def _():
        h_scr[...] = _norm_mod(x_ref[...], g_ref[...], sh_ref[...], sc_ref[...]).astype(BF16)

    acc = _dot(h_scr[...], w_ref[...])

    def emit(dst, rotate):
        for gi in range(groups):
            a = acc[:, gi * HEAD_DIM:(gi + 1) * HEAD_DIM]
            if rotate:
                a = _rope(a, cos_ref[...], sin_ref[...])
            dst[gi] = a.astype(dst.dtype)

    @pl.when(j == 0)
    def _():
        emit(cmp_ref, True)

    @pl.when(j == 1)
    def _():
        emit(cmp_ref, False)

    @pl.when((j >= 2) & (j % 2 == 0))
    def _():
        emit(rest_ref, True)

    @pl.when((j >= 2) & (j % 2 == 1))
    def _():
        emit(rest_ref, False)


def _kv_proj(x, g, shift, scale, w, cos, sin, groups):
    bsz, s, d = x.shape
    tm = _tile(s, 1024)
    tn = groups * HEAD_DIM
    row = lambda bi, i, j: (bi, 0, 0)
    tab = pl.BlockSpec((tm, HEAD_DIM), lambda bi, i, j: (i, 0))
    return pl.pallas_call(
        functools.partial(_kv_kernel, groups=groups),
        grid=(bsz, s // tm, 2 * N_BRANCHES),
        in_specs=[
            pl.BlockSpec((None, tm, d), lambda bi, i, j: (bi, i, 0)),
            pl.BlockSpec((1, d), lambda bi, i, j: (0, 0)),
            pl.BlockSpec((None, 1, d), row), pl.BlockSpec((None, 1, d), row),
            pl.BlockSpec((d, tn), lambda bi, i, j: (0, j)),
            tab, tab,
        ],
        out_specs=[
            pl.BlockSpec((None, None, groups, tm, HEAD_DIM),
                         lambda bi, i, j: (bi, jnp.minimum(j, 1), 0, i, 0)),
            pl.BlockSpec((None, None, groups, tm, HEAD_DIM),
                         lambda bi, i, j: (bi, jnp.maximum(j - 2, 0), 0, i, 0)),
        ],
        out_shape=[jax.ShapeDtypeStruct((bsz, 2, groups, s, HEAD_DIM), F32),
                   jax.ShapeDtypeStruct((bsz, 4, groups, s, HEAD_DIM), BF16)],
        scratch_shapes=[pltpu.VMEM((tm, d), BF16)],
        compiler_params=_params(("parallel", "parallel", "arbitrary")),
        name="kv_proj",
    )(x, g.reshape(1, d), shift, scale, w, cos, sin)


def _qg_kernel(x_ref, g_ref, sh_ref, sc_ref, w_ref, cos_ref, sin_ref, q_ref, gate_ref, h_scr,
               *, heads_per_tile, n_q):
    j = pl.program_id(2)

    @pl.when(j == 0)
    def _():
        h_scr[...] = _norm_mod(x_ref[...], g_ref[...], sh_ref[...], sc_ref[...]).astype(BF16)

    acc = _dot(h_scr[...], w_ref[...])

    @pl.when(j < n_q)
    def _():
        for hi in range(heads_per_tile):
            cols = slice(hi * HEAD_DIM, (hi + 1) * HEAD_DIM)
            a = _rope(acc[:, cols], cos_ref[...], sin_ref[...]) * (HEAD_DIM ** -0.5)
            q_ref[:, cols] = a.astype(BF16)

    @pl.when(j >= n_q)
    def _():
        gate_ref[...] = jax.nn.sigmoid(acc)


def _qg_proj(x, g, shift, scale, w_q, w_gate, cos, sin, tn):
    bsz, s, d = x.shape
    nq_cols, ng_cols = w_q.shape[1], w_gate.shape[1]
    n_q, n_g = nq_cols // tn, ng_cols // tn
    tm = _tile(s, 1024)
    row = lambda bi, i, j: (bi, 0, 0)
    tab = pl.BlockSpec((tm, HEAD_DIM), lambda bi, i, j: (i, 0))
    w = jnp.concatenate([w_q, w_gate], axis=1)
    return pl.pallas_call(
        functools.partial(_qg_kernel, heads_per_tile=tn // HEAD_DIM, n_q=n_q),
        grid=(bsz, s // tm, n_q + n_g),
        in_specs=[
            pl.BlockSpec((None, tm, d), lambda bi, i, j: (bi, i, 0)),
            pl.BlockSpec((1, d), lambda bi, i, j: (0, 0)),
            pl.BlockSpec((None, 1, d), row), pl.BlockSpec((None, 1, d), row),
            pl.BlockSpec((d, tn), lambda bi, i, j: (0, j)),
            tab, tab,
        ],
        out_specs=[
            pl.BlockSpec((None, tm, tn), lambda bi, i, j: (bi, i, jnp.minimum(j, n_q - 1))),
            pl.BlockSpec((None, tm, tn), lambda bi, i, j: (bi, i, jnp.maximum(j - n_q, 0))),
        ],
        out_shape=[jax.ShapeDtypeStruct((bsz, s, nq_cols), BF16),
                   jax.ShapeDtypeStruct((bsz, s, ng_cols), F32)],
        scratch_shapes=[pltpu.VMEM((tm, d), BF16)],
        compiler_params=_params(("parallel", "parallel", "arbitrary")),
        name="qg_proj",
    )(x, g.reshape(1, d), shift, scale, w, cos, sin)


def _compress_kernel(h_ref, pos_ref, w1_ref, w2_ref, o_ref):
    hv = h_ref[...]
    half = hv.shape[1]
    lo = (hv + pos_ref[0:1, :]).astype(BF16)
    hi = (hv + pos_ref[1:2, :]).astype(BF16)
    first = _dot(lo, w1_ref[0:half, :])
    second = _dot(hi, w1_ref[half:2 * half, :])
    n_rows = hv.shape[0]
    pre = first + pltpu.roll(second, n_rows - 1, 0)
    o_ref[...] = _dot(jax.nn.gelu(pre).astype(BF16), w2_ref[...]).astype(BF16)


def _compress(kv_cmp, pos, w1, w2):
    bsz, _, groups, s, _ = kv_cmp.shape
    n_rows = s // CMP_STRIDE
    flat = CMP_STRIDE * HEAD_DIM
    hidden = w1.shape[2]
    h = kv_cmp.reshape(bsz, 2, groups, n_rows, flat)
    return pl.pallas_call(
        _compress_kernel,
        grid=(bsz, 2, groups),
        in_specs=[
            pl.BlockSpec((None, None, None, n_rows, flat), lambda bi, t, gi: (bi, t, gi, 0, 0)),
            pl.BlockSpec((None, 2, flat), lambda bi, t, gi: (t, 0, 0)),
            pl.BlockSpec((None, 2 * flat, hidden), lambda bi, t, gi: (t, 0, 0)),
            pl.BlockSpec((None, hidden, HEAD_DIM), lambda bi, t, gi: (t, 0, 0)),
        ],
        out_specs=pl.BlockSpec((None, None, None, n_rows, HEAD_DIM),
                               lambda bi, t, gi: (bi, t, gi, 0, 0)),
        out_shape=jax.ShapeDtypeStruct((bsz, 2, groups, n_rows, HEAD_DIM), BF16),
        compiler_params=_params(("parallel", "parallel", "parallel")),
        name="compress",
    )(h, pos, w1, w2)


def _masked_softmax(s, mask):
    s = jnp.where(mask, s, NEG_INF)
    m = jnp.max(s, axis=-1, keepdims=True)
    e = jnp.where(mask, jnp.exp(s - m), 0.0)
    den = jnp.sum(e, axis=-1, keepdims=True)
    return e / jnp.maximum(den, 1e-30)


def _attn_kernel(q_ref, gate_ref, kc_ref, vc_ref, ks_ref, vs_ref, kw_ref, vw_ref, ovt_ref, ex_ref,
                 o_ref, q4_scr, pick_scr, s_scr, m_scr, acc_scr, *, hg, n_cmp, n_slc, n_sel):
    tq = ATTN_TILE
    tk = SLC_TILE
    i = pl.program_id(2)
    t0 = i * tq
    for h in range(hg):
        q4_scr[h * tq:(h + 1) * tq, :] = q_ref[:, h * HEAD_DIM:(h + 1) * HEAD_DIM]
    tpos = t0 + lax.broadcasted_iota(jnp.int32, (tq, 1), 0)
    tpos4 = jnp.concatenate([tpos] * hg, axis=0)
    lane = lax.broadcasted_iota(jnp.int32, (1, LANES), 1)

    s_cmp = _dot_nt(q4_scr[...], kc_ref[...])
    valid = (lane * CMP_STRIDE + (CMP_BLOCK - 1) <= tpos4) & (lane < n_cmp)
    p_cmp = _masked_softmax(s_cmp, valid).astype(BF16)
    o_cmp = _dot(p_cmp, vc_ref[...])

    p_heads = jnp.concatenate([p_cmp[h * tq:(h + 1) * tq] for h in range(hg)], axis=1)
    imp = _dot_nt(ovt_ref[...], p_heads)[:n_slc]
    blk = lax.broadcasted_iota(jnp.int32, (n_slc, 1), 0)
    tpos_l = t0 + lax.broadcasted_iota(jnp.int32, (1, tq), 1)
    cur = tpos_l // SEL_BLOCK
    forced = (blk == 0) | (blk == cur) | (blk == cur - 1)
    imp = jnp.where(blk * SEL_BLOCK <= tpos_l, imp + jnp.where(forced, FORCE_BONUS, 0.0), NEG_INF)
    rank = jnp.zeros((n_slc, tq), jnp.int32)
    for c in range(n_slc):
        row = imp[c:c + 1, :]
        beats = (row > imp) | ((row == imp) & (c < blk))
        rank = rank + jnp.where(beats, 1, 0)
    sel = jnp.where(rank < n_sel, 1.0, 0.0)
    sel = jnp.concatenate([sel, jnp.zeros((LANES - n_slc, tq), F32)], axis=0).T.astype(BF16)
    pick_scr[...] = _dot(sel, ex_ref[...])

    def with_ones(v):
        return jnp.concatenate([v, jnp.ones(v.shape, BF16)], axis=1)

    def normalised(acc):
        return acc[:, :HEAD_DIM] / jnp.maximum(acc[:, HEAD_DIM:], 1e-30)

    n_tiles = i // (tk // tq) + 1
    m_scr[...] = jnp.full_like(m_scr, NEG_INF)

    def scores(kt, carry):
        k0 = pl.multiple_of(kt * tk, tk)
        kpos = k0 + lax.broadcasted_iota(jnp.int32, (1, tk), 1)
        bias = jnp.where((pick_scr[:, pl.ds(k0, tk)] > 0.5) & (kpos <= tpos), 0.0, NEG_INF)
        s4 = _dot_nt(q4_scr[...], ks_ref[pl.ds(k0, tk), :])
        for h in range(hg):
            rows = slice(h * tq, (h + 1) * tq)
            sh = s4[rows] + bias
            s_scr[rows, pl.ds(k0, tk)] = sh
            mx = sh[:, :LANES]
            for lt in range(1, tk // LANES):
                mx = jnp.maximum(mx, sh[:, lt * LANES:(lt + 1) * LANES])
            m_scr[rows] = jnp.maximum(m_scr[rows], mx)
        return carry

    lax.fori_loop(0, n_tiles, scores, 0)
    m_scr[...] = jnp.broadcast_to(jnp.max(m_scr[...], axis=-1, keepdims=True), m_scr.shape)
    acc_scr[...] = jnp.zeros_like(acc_scr)

    def weighted(kt, carry):
        k0 = pl.multiple_of(kt * tk, tk)
        m = jnp.concatenate([m_scr[...]] * (tk // LANES), axis=1)
        e = jnp.exp(s_scr[:, pl.ds(k0, tk)] - m).astype(BF16)
        acc_scr[...] += _dot(e, with_ones(vs_ref[pl.ds(k0, tk), :]))
        return carry

    lax.fori_loop(0, n_tiles, weighted, 0)
    o_slc = normalised(acc_scr[...])

    wk = WINDOW + tq
    kw0 = pl.multiple_of(jnp.maximum(t0 - WINDOW, 0), tq)
    kpos = kw0 + lax.broadcasted_iota(jnp.int32, (1, wk), 1)
    bias = jnp.where((kpos <= tpos) & (kpos > tpos - WINDOW), 0.0, NEG_INF)
    s4 = _dot_nt(q4_scr[...], kw_ref[pl.ds(kw0, wk), :])
    v_win = with_ones(vw_ref[pl.ds(kw0, wk), :])

    gates = gate_ref[...]
    for h in range(hg):
        rows = slice(h * tq, (h + 1) * tq)
        sh = s4[rows] + bias
        e = jnp.exp(sh - jnp.max(sh, axis=-1, keepdims=True)).astype(BF16)
        o_win = normalised(_dot(e, v_win))
        c0 = h * N_BRANCHES
        o = (gates[:, c0:c0 + 1] * o_cmp[rows] + gates[:, c0 + 1:c0 + 2] * o_slc[rows]
             + gates[:, c0 + 2:c0 + 3] * o_win)
        o_ref[:, h * HEAD_DIM:(h + 1) * HEAD_DIM] = o.astype(BF16)


def _attention(q, gates, kv_cmp, kv_rest, hg):
    bsz, s, _ = q.shape
    groups = kv_rest.shape[2]
    n_rows = kv_cmp.shape[3]
    n_cmp = n_rows - 1
    n_slc = s // SEL_BLOCK
    n_sel = min(N_SELECT, n_slc)
    tq = ATTN_TILE
    assert n_rows == LANES and n_slc <= LANES and n_slc % 8 == 0
    assert s % SLC_TILE == 0 and s >= WINDOW + tq

    ci = np.arange(LANES)[None, :]
    sj = np.arange(LANES)[:, None]
    overlap_t = ((ci * CMP_STRIDE <= sj * SEL_BLOCK + SEL_BLOCK - 1)
                 & (ci * CMP_STRIDE + CMP_BLOCK - 1 >= sj * SEL_BLOCK) & (ci < n_cmp) & (sj < n_slc))
    overlap_t = jnp.asarray(np.tile(overlap_t, (1, hg)), BF16)
    expand = jnp.asarray(np.arange(s)[None, :] // SEL_BLOCK == np.arange(LANES)[:, None], BF16)

    def kv_spec(which):
        return pl.BlockSpec((None, None, None, s, HEAD_DIM), lambda bi, gi, i: (bi, which, gi, 0, 0))

    def cmp_spec(which):
        return pl.BlockSpec((None, None, None, n_rows, HEAD_DIM), lambda bi, gi, i: (bi, which, gi, 0, 0))

    qblk = pl.BlockSpec((None, tq, hg * HEAD_DIM), lambda bi, gi, i: (bi, i, gi))
    return pl.pallas_call(
        functools.partial(_attn_kernel, hg=hg, n_cmp=n_cmp, n_slc=n_slc, n_sel=n_sel),
        grid=(bsz, groups, s // tq),
        in_specs=[
            qblk,
            pl.BlockSpec((None, tq, LANES), lambda bi, gi, i: (bi, i, gi)),
            cmp_spec(0), cmp_spec(1),
            kv_spec(0), kv_spec(1), kv_spec(2), kv_spec(3),
            pl.BlockSpec((LANES, hg * LANES), lambda bi, gi, i: (0, 0)),
            pl.BlockSpec((LANES, s), lambda bi, gi, i: (0, 0)),
        ],
        out_specs=qblk,
        out_shape=jax.ShapeDtypeStruct(q.shape, BF16),
        scratch_shapes=[pltpu.VMEM((hg * tq, HEAD_DIM), BF16),
                        pltpu.VMEM((tq, s), F32),
                        pltpu.VMEM((hg * tq, s), F32),
                        pltpu.VMEM((hg * tq, LANES), F32),
                        pltpu.VMEM((hg * tq, 2 * HEAD_DIM), F32)],
        compiler_params=_params(("parallel", "parallel", "arbitrary")),
        name="nsa_attention",
    )(q, gates, kv_cmp, kv_cmp, kv_rest, kv_rest, kv_rest, kv_rest, overlap_t, expand)


def _rope_tables(s):
    freqs = ROPE_THETA ** (-jnp.arange(HALF_HEAD, dtype=F32) / HALF_HEAD)
    ang = jnp.arange(s).astype(F32)[:, None] * freqs[None, :]
    cos, sin = jnp.cos(ang), jnp.sin(ang)
    return jnp.concatenate([cos, cos], axis=1), jnp.concatenate([-sin, sin], axis=1)


def kernel(x, c, mod_w, mod_b, norm_g, mlp_w1, mlp_w2, a_w_in, a_b_in, a_ln_g, a_ln_b, a_w_s, a_b_s, a_w_out, a_b_out, kv_norm_g, kv_mod_w, kv_mod_b, w_kv, cmp_pos_k, cmp_w1_k, cmp_w2_k, cmp_pos_v, cmp_w1_v, cmp_w2_v, b_w_qg, b_w_o, final_g):
    bsz, s, d = x.shape
    depth = mod_w.shape[0]
    n_a = a_w_in.shape[0]
    groups = w_kv.shape[1] // (2 * N_BRANCHES * HEAD_DIM)
    heads = b_w_o.shape[1] // HEAD_DIM
    hg = heads // groups
    assert hg * N_BRANCHES <= LANES

    mod = _cond_matmul(c, mod_w.reshape(depth * 2, d, 3 * d), mod_b.reshape(depth * 2, 3 * d))
    mod = mod.reshape(depth, 2, bsz, 3, 1, d)
    kv_mod = _cond_matmul(c, kv_mod_w[None], kv_mod_b[None]).reshape(bsz, 2, 1, d)
    cos, sin = _rope_tables(s)

    for layer in range(depth):
        shift, scale, gate = (mod[layer, 0, :, t] for t in range(3))
        if layer < n_a:
            u, v = _gmlp_in(x, norm_g[layer, 0], shift, scale,
                            a_w_in[layer].astype(BF16), a_b_in[layer])
            mixed = _gmlp_gate(u, v, a_ln_g[layer], a_ln_b[layer], a_w_s[layer], a_b_s[layer])
            x = _mm_residual(mixed, a_w_out[layer].astype(BF16), a_b_out[layer], x, gate)
        else:
            j = layer - n_a
            if layer == n_a:
                kv_cmp, kv_rest = _kv_proj(x, kv_norm_g, kv_mod[:, 0], kv_mod[:, 1],
                                           w_kv.astype(BF16), cos, sin, groups)
                pos = jnp.stack([cmp_pos_k, cmp_pos_v]).reshape(2, 2, CMP_STRIDE * HEAD_DIM)
                kv_cmp = _compress(kv_cmp, pos,
                                   jnp.stack([cmp_w1_k, cmp_w1_v]).astype(BF16),
                                   jnp.stack([cmp_w2_k, cmp_w2_v]).astype(BF16))
            w_qg = b_w_qg[j]
            w_q = w_qg[:, :heads * HEAD_DIM]
            w_gate = w_qg[:, heads * HEAD_DIM:].reshape(d, groups, hg * N_BRANCHES)
            w_gate = jnp.pad(w_gate, ((0, 0), (0, 0), (0, LANES - hg * N_BRANCHES)))
            w_gate = w_gate.reshape(d, groups * LANES)
            tn = hg * HEAD_DIM
            pad = (-w_gate.shape[1]) % tn
            w_gate = jnp.pad(w_gate, ((0, 0), (0, pad)))
            q, gates = _qg_proj(x, norm_g[layer, 0], shift, scale,
                                w_q.astype(BF16), w_gate.astype(BF16), cos, sin, tn)
            o = _attention(q, gates, kv_cmp, kv_rest, hg)
            x = _mm_residual(o, b_w_o[j].astype(BF16), jnp.zeros((d,), F32), x, gate)

        shift, scale, gate = (mod[layer, 1, :, t] for t in range(3))
        x = _mlp(x, norm_g[layer, 1], shift, scale, gate,
                 mlp_w1[layer].astype(BF16), mlp_w2[layer].astype(BF16),
                 final_g, final_norm=(layer == depth - 1))
    return x
```

```python
import functools

import numpy as np
import jax
import jax.numpy as jnp
from jax import lax
from jax.experimental import pallas as pl
from jax.experimental.pallas import tpu as pltpu

BF16 = jnp.bfloat16
F32 = jnp.float32

LANES = 128
MXU_COLS = 256
HEAD_DIM = 128
HALF_HEAD = HEAD_DIM // 2
N_BRANCHES = 3
CMP_BLOCK = 32
CMP_STRIDE = 16
SEL_BLOCK = 64
N_SELECT = 8
WINDOW = 512
ROPE_THETA = 10000.0
EPS = 1e-6
NEG_INF = -1e30
FORCE_BONUS = 1e6
ATTN_TILE = 128
SLC_TILE = 512
VMEM_LIMIT = 56 * 1024 * 1024


def _dot(a, b):
    return jnp.dot(a, b, preferred_element_type=F32)


def _dot_nt(a, b):
    return lax.dot_general(a, b, (((1,), (1,)), ((), ())), preferred_element_type=F32)


def _tile(n, pref):
    return pref if n % pref == 0 else n


def _params(semantics):
    return pltpu.CompilerParams(dimension_semantics=semantics, vmem_limit_bytes=VMEM_LIMIT)


def _rms(x, g):
    return (x * lax.rsqrt(jnp.mean(x * x, axis=-1, keepdims=True) + EPS)) * g


def _norm_mod(x, g, shift, scale):
    return _rms(x, g) * (1.0 + scale) + shift


def _rope(a, cos, sin):
    return a * cos + pltpu.roll(a, HALF_HEAD, 1) * sin


def _dot_chunks(h_ref, w_ref, epilogue):
    n = w_ref.shape[1]
    chunk = MXU_COLS if n % MXU_COLS == 0 else n
    for c0 in range(0, n, chunk):
        cols = slice(c0, c0 + chunk)
        epilogue(cols, _dot(h_ref[...], w_ref[:, cols]))


def _cond_kernel(c_ref, w_ref, b_ref, o_ref):
    cond = jax.nn.silu(c_ref[...]).astype(BF16)
    o_ref[...] = _dot(cond, w_ref[...].astype(BF16)) + b_ref[...]


def _cond_matmul(c, w, b):
    n, d, e = w.shape
    bsz = c.shape[0]
    te = _tile(e, 1024)
    return pl.pallas_call(
        _cond_kernel,
        grid=(n, e // te),
        in_specs=[
            pl.BlockSpec((bsz, d), lambda s, j: (0, 0)),
            pl.BlockSpec((None, d, te), lambda s, j: (s, 0, j)),
            pl.BlockSpec((None, 1, te), lambda s, j: (s, 0, j)),
        ],
        out_specs=pl.BlockSpec((None, bsz, te), lambda s, j: (s, 0, j)),
        out_shape=jax.ShapeDtypeStruct((n, bsz, e), F32),
        compiler_params=_params(("parallel", "parallel")),
        name="cond_matmul",
    )(c, w, b.reshape(n, 1, e))


def _gmlp_in_kernel(x_ref, g_ref, sh_ref, sc_ref, w_ref, b_ref, u_ref, v_ref, h_scr, *, n_u):
    j = pl.program_id(2)

    @pl.when(j == 0)
    def _():
        h_scr[...] = _norm_mod(x_ref[...], g_ref[...], sh_ref[...], sc_ref[...]).astype(BF16)

    def emit(dst):
        def store(cols, acc):
            dst[:, cols] = jax.nn.gelu(acc + b_ref[:, cols])
        _dot_chunks(h_scr, w_ref, store)

    @pl.when(j < n_u)
    def _():
        emit(u_ref)

    @pl.when(j >= n_u)
    def _():
        emit(v_ref)


def _gmlp_in(x, g, shift, scale, w, b):
    bsz, s, d = x.shape
    n = w.shape[1]
    width = n // 2
    tm = _tile(s, 1024)
    tn = _tile(width, 1024)
    n_u = width // tn
    row = lambda bi, i, j: (bi, 0, 0)
    return pl.pallas_call(
        functools.partial(_gmlp_in_kernel, n_u=n_u),
        grid=(bsz, s // tm, n // tn),
        in_specs=[
            pl.BlockSpec((None, tm, d), lambda bi, i, j: (bi, i, 0)),
            pl.BlockSpec((1, d), lambda bi, i, j: (0, 0)),
            pl.BlockSpec((None, 1, d), row),
            pl.BlockSpec((None, 1, d), row),
            pl.BlockSpec((d, tn), lambda bi, i, j: (0, j)),
            pl.BlockSpec((1, tn), lambda bi, i, j: (0, j)),
        ],
        out_specs=[
            pl.BlockSpec((None, tm, tn), lambda bi, i, j: (bi, i, jnp.minimum(j, n_u - 1))),
            pl.BlockSpec((None, tm, tn), lambda bi, i, j: (bi, i, jnp.maximum(j - n_u, 0))),
        ],
        out_shape=[jax.ShapeDtypeStruct((bsz, s, width), F32)] * 2,
        scratch_shapes=[pltpu.VMEM((tm, d), BF16)],
        compiler_params=_params(("parallel", "parallel", "arbitrary")),
        name="gmlp_in",
    )(x, g.reshape(1, d), shift, scale, w, b.reshape(1, n))


def _gmlp_gate_kernel(u_ref, v_ref, lg_ref, lb_ref, ws_ref, bs_ref, o_ref, *, groups, gdim):
    v = v_ref[...]
    mu = jnp.mean(v, axis=-1, keepdims=True)
    dv = v - mu
    var = jnp.mean(dv * dv, axis=-1, keepdims=True)
    vn = ((dv * lax.rsqrt(var + EPS)) * lg_ref[...] + lb_ref[...]).astype(BF16)
    chunk = v.shape[0]
    causal = (lax.broadcasted_iota(jnp.int32, (chunk, chunk), 1)
              <= lax.broadcasted_iota(jnp.int32, (chunk, chunk), 0))
    for gi in range(groups):
        cols = slice(gi * gdim, (gi + 1) * gdim)
        w = jnp.where(causal, ws_ref[gi], 0.0).astype(BF16)
        mixed = _dot(w, vn[:, cols]) + bs_ref[:, gi:gi + 1]
        o_ref[:, cols] = (u_ref[:, cols] * mixed).astype(BF16)


def _gmlp_gate(u, v, ln_g, ln_b, w_s, b_s):
    bsz, s, width = u.shape
    groups, chunk, _ = w_s.shape
    gdim = width // groups
    blk = pl.BlockSpec((None, chunk, width), lambda bi, i: (bi, i, 0))
    vec = pl.BlockSpec((1, width), lambda bi, i: (0, 0))
    return pl.pallas_call(
        functools.partial(_gmlp_gate_kernel, groups=groups, gdim=gdim),
        grid=(bsz, s // chunk),
        in_specs=[blk, blk, vec, vec,
                  pl.BlockSpec((groups, chunk, chunk), lambda bi, i: (0, 0, 0)),
                  pl.BlockSpec((chunk, groups), lambda bi, i: (0, 0))],
        out_specs=blk,
        out_shape=jax.ShapeDtypeStruct((bsz, s, width), BF16),
        compiler_params=_params(("parallel", "parallel")),
        name="gmlp_gate",
    )(u, v, ln_g.reshape(1, width), ln_b.reshape(1, width), w_s, b_s.T)


def _mm_res_kernel(a_ref, w_ref, b_ref, x_ref, gate_ref, o_ref):
    y = _dot(a_ref[...], w_ref[...]) + b_ref[...]
    o_ref[...] = x_ref[...] + gate_ref[...] * y


def _mm_residual(a, w, b, x, gate):
    bsz, s, k = a.shape
    d = w.shape[1]
    tm = _tile(s, 1024)
    tn = _tile(d, 512)
    return pl.pallas_call(
        _mm_res_kernel,
        grid=(bsz, s // tm, d // tn),
        in_specs=[
            pl.BlockSpec((None, tm, k), lambda bi, i, j: (bi, i, 0)),
            pl.BlockSpec((k, tn), lambda bi, i, j: (0, j)),
            pl.BlockSpec((1, tn), lambda bi, i, j: (0, j)),
            pl.BlockSpec((None, tm, tn), lambda bi, i, j: (bi, i, j)),
            pl.BlockSpec((None, 1, tn), lambda bi, i, j: (bi, 0, j)),
        ],
        out_specs=pl.BlockSpec((None, tm, tn), lambda bi, i, j: (bi, i, j)),
        out_shape=jax.ShapeDtypeStruct((bsz, s, d), F32),
        compiler_params=_params(("parallel", "parallel", "parallel")),
        name="matmul_residual",
    )(a, w, b.reshape(1, d), x, gate)


def _mlp_kernel(x_ref, g_ref, sh_ref, sc_ref, gate_ref, w1_ref, w2_ref, fg_ref, o_ref,
                h_scr, acc_scr, *, final_norm):
    f = pl.program_id(2)

    @pl.when(f == 0)
    def _():
        h_scr[...] = _norm_mod(x_ref[...], g_ref[...], sh_ref[...], sc_ref[...]).astype(BF16)
        acc_scr[...] = jnp.zeros_like(acc_scr)

    a = jnp.square(jnp.maximum(_dot(h_scr[...], w1_ref[...]), 0.0)).astype(BF16)
    acc_scr[...] += _dot(a, w2_ref[...])

    @pl.when(f == pl.num_programs(2) - 1)
    def _():
        xn = x_ref[...] + gate_ref[...] * acc_scr[...]
        if final_norm:
            xn = _rms(xn, fg_ref[...])
        o_ref[...] = xn


def _mlp(x, g, shift, scale, gate, w1, w2, final_g, final_norm):
    bsz, s, d = x.shape
    dff = w1.shape[1]
    tm = _tile(s, 512)
    tf = _tile(dff, 1024)
    row = lambda bi, i, f: (bi, 0, 0)
    vec = pl.BlockSpec((1, d), lambda bi, i, f: (0, 0))
    xblk = pl.BlockSpec((None, tm, d), lambda bi, i, f: (bi, i, 0))
    return pl.pallas_call(
        functools.partial(_mlp_kernel, final_norm=final_norm),
        grid=(bsz, s // tm, dff // tf),
        in_specs=[
            xblk, vec,
            pl.BlockSpec((None, 1, d), row), pl.BlockSpec((None, 1, d), row),
            pl.BlockSpec((None, 1, d), row),
            pl.BlockSpec((d, tf), lambda bi, i, f: (0, f)),
            pl.BlockSpec((tf, d), lambda bi, i, f: (f, 0)),
            vec,
        ],
        out_specs=xblk,
        out_shape=jax.ShapeDtypeStruct((bsz, s, d), F32),
        scratch_shapes=[pltpu.VMEM((tm, d), BF16), pltpu.VMEM((tm, d), F32)],
        compiler_params=_params(("parallel", "parallel", "arbitrary")),
        name="mlp",
    )(x, g.reshape(1, d), shift, scale, gate, w1, w2, final_g.reshape(1, d))


def _kv_kernel(x_ref, g_ref, sh_ref, sc_ref, w_ref, cos_ref, sin_ref, cmp_ref, rest_ref, h_scr,
               *, groups):
    j = pl.program_id(2)

    @pl.when(j == 0)
    def _():
        h_scr[...] = _norm_mod(x_ref[...], g_ref[...], sh_ref[...], sc_ref[...]).astype(BF16)

    def emit(dst, rotate):
        def store(cols, acc):
            for c0 in range(0, acc.shape[1], HEAD_DIM):
                a = acc[:, c0:c0 + HEAD_DIM]
                if rotate:
                    a = _rope(a, cos_ref[...], sin_ref[...])
                dst[(cols.start + c0) // HEAD_DIM] = a.astype(dst.dtype)
        _dot_chunks(h_scr, w_ref, store)

    @pl.when(j == 0)
    def _():
        emit(cmp_ref, True)

    @pl.when(j == 1)
    def _():
        emit(cmp_ref, False)

    @pl.when((j >= 2) & (j % 2 == 0))
    def _():
        emit(rest_ref, True)

    @pl.when((j >= 2) & (j % 2 == 1))
    def _():
        emit(rest_ref, False)


def _kv_proj(x, g, shift, scale, w, cos, sin, groups):
    bsz, s, d = x.shape
    tm = _tile(s, 1024)
    tn = groups * HEAD_DIM
    row = lambda bi, i, j: (bi, 0, 0)
    tab = pl.BlockSpec((tm, HEAD_DIM), lambda bi, i, j: (i, 0))
    return pl.pallas_call(
        functools.partial(_kv_kernel, groups=groups),
        grid=(bsz, s // tm, 2 * N_BRANCHES),
        in_specs=[
            pl.BlockSpec((None, tm, d), lambda bi, i, j: (bi, i, 0)),
            pl.BlockSpec((1, d), lambda bi, i, j: (0, 0)),
            pl.BlockSpec((None, 1, d), row), pl.BlockSpec((None, 1, d), row),
            pl.BlockSpec((d, tn), lambda bi, i, j: (0, j)),
            tab, tab,
        ],
        out_specs=[
            pl.BlockSpec((None, None, groups, tm, HEAD_DIM),
                         lambda bi, i, j: (bi, jnp.minimum(j, 1), 0, i, 0)),
            pl.BlockSpec((None, None, groups, tm, HEAD_DIM),
                         lambda bi, i, j: (bi, jnp.maximum(j - 2, 0), 0, i, 0)),
        ],
        out_shape=[jax.ShapeDtypeStruct((bsz, 2, groups, s, HEAD_DIM), F32),
                   jax.ShapeDtypeStruct((bsz, 4, groups, s, HEAD_DIM), BF16)],
        scratch_shapes=[pltpu.VMEM((tm, d), BF16)],
        compiler_params=_params(("parallel", "parallel", "arbitrary")),
        name="kv_proj",
    )(x, g.reshape(1, d), shift, scale, w, cos, sin)


def _qg_kernel(x_ref, g_ref, sh_ref, sc_ref, w_ref, cos_ref, sin_ref, q_ref, gate_ref, h_scr,
               *, heads_per_tile, n_q):
    j = pl.program_id(2)

    @pl.when(j == 0)
    def _():
        h_scr[...] = _norm_mod(x_ref[...], g_ref[...], sh_ref[...], sc_ref[...]).astype(BF16)

    @pl.when(j < n_q)
    def _():
        def store(cols, acc):
            for c0 in range(0, acc.shape[1], HEAD_DIM):
                a = _rope(acc[:, c0:c0 + HEAD_DIM], cos_ref[...], sin_ref[...]) * (HEAD_DIM ** -0.5)
                q_ref[:, cols.start + c0:cols.start + c0 + HEAD_DIM] = a.astype(BF16)
        _dot_chunks(h_scr, w_ref, store)

    @pl.when(j >= n_q)
    def _():
        def store(cols, acc):
            gate_ref[:, cols] = jax.nn.sigmoid(acc)
        _dot_chunks(h_scr, w_ref, store)


def _qg_proj(x, g, shift, scale, w_q, w_gate, cos, sin, tn):
    bsz, s, d = x.shape
    nq_cols, ng_cols = w_q.shape[1], w_gate.shape[1]
    n_q, n_g = nq_cols // tn, ng_cols // tn
    tm = _tile(s, 1024)
    row = lambda bi, i, j: (bi, 0, 0)
    tab = pl.BlockSpec((tm, HEAD_DIM), lambda bi, i, j: (i, 0))
    w = jnp.concatenate([w_q, w_gate], axis=1)
    return pl.pallas_call(
        functools.partial(_qg_kernel, heads_per_tile=tn // HEAD_DIM, n_q=n_q),
        grid=(bsz, s // tm, n_q + n_g),
        in_specs=[
            pl.BlockSpec((None, tm, d), lambda bi, i, j: (bi, i, 0)),
            pl.BlockSpec((1, d), lambda bi, i, j: (0, 0)),
            pl.BlockSpec((None, 1, d), row), pl.BlockSpec((None, 1, d), row),
            pl.BlockSpec((d, tn), lambda bi, i, j: (0, j)),
            tab, tab,
        ],
        out_specs=[
            pl.BlockSpec((None, tm, tn), lambda bi, i, j: (bi, i, jnp.minimum(j, n_q - 1))),
            pl.BlockSpec((None, tm, tn), lambda bi, i, j: (bi, i, jnp.maximum(j - n_q, 0))),
        ],
        out_shape=[jax.ShapeDtypeStruct((bsz, s, nq_cols), BF16),
                   jax.ShapeDtypeStruct((bsz, s, ng_cols), F32)],
        scratch_shapes=[pltpu.VMEM((tm, d), BF16)],
        compiler_params=_params(("parallel", "parallel", "arbitrary")),
        name="qg_proj",
    )(x, g.reshape(1, d), shift, scale, w, cos, sin)


def _compress_kernel(h_ref, pos_ref, w1_ref, w2_ref, o_ref):
    hv = h_ref[...]
    half = hv.shape[1]
    lo = (hv + pos_ref[0:1, :]).astype(BF16)
    hi = (hv + pos_ref[1:2, :]).astype(BF16)
    first = _dot(lo, w1_ref[0:half, :])
    second = _dot(hi, w1_ref[half:2 * half, :])
    n_rows = hv.shape[0]
    pre = first + pltpu.roll(second, n_rows - 1, 0)
    o_ref[...] = _dot(jax.nn.gelu(pre).astype(BF16), w2_ref[...]).astype(BF16)


def _compress(kv_cmp, pos, w1, w2):
    bsz, _, groups, s, _ = kv_cmp.shape
    n_rows = s // CMP_STRIDE
    flat = CMP_STRIDE * HEAD_DIM
    hidden = w1.shape[2]
    h = kv_cmp.reshape(bsz, 2, groups, n_rows, flat)
    return pl.pallas_call(
        _compress_kernel,
        grid=(bsz, 2, groups),
        in_specs=[
            pl.BlockSpec((None, None, None, n_rows, flat), lambda bi, t, gi: (bi, t, gi, 0, 0)),
            pl.BlockSpec((None, 2, flat), lambda bi, t, gi: (t, 0, 0)),
            pl.BlockSpec((None, 2 * flat, hidden), lambda bi, t, gi: (t, 0, 0)),
            pl.BlockSpec((None, hidden, HEAD_DIM), lambda bi, t, gi: (t, 0, 0)),
        ],
        out_specs=pl.BlockSpec((None, None, None, n_rows, HEAD_DIM),
                               lambda bi, t, gi: (bi, t, gi, 0, 0)),
        out_shape=jax.ShapeDtypeStruct((bsz, 2, groups, n_rows, HEAD_DIM), BF16),
        compiler_params=_params(("parallel", "parallel", "parallel")),
        name="compress",
    )(h, pos, w1, w2)


def _masked_softmax(s, mask):
    s = jnp.where(mask, s, NEG_INF)
    m = jnp.max(s, axis=-1, keepdims=True)
    e = jnp.where(mask, jnp.exp(s - m), 0.0)
    den = jnp.sum(e, axis=-1, keepdims=True)
    return e / jnp.maximum(den, 1e-30)


def _attn_kernel(q_ref, gate_ref, kc_ref, vc_ref, ks_ref, vs_ref, kw_ref, vw_ref, ovt_ref, ex_ref,
                 o_ref, q4_scr, pick_scr, s_scr, m_scr, acc_scr, oc_scr, ow_scr,
                 *, hg, n_cmp, n_slc, n_sel):
    tq = ATTN_TILE
    tk = SLC_TILE
    i = pl.program_id(2)
    t0 = i * tq
    for h in range(hg):
        q4_scr[h * tq:(h + 1) * tq, :] = q_ref[:, h * HEAD_DIM:(h + 1) * HEAD_DIM]
    tpos = t0 + lax.broadcasted_iota(jnp.int32, (tq, 1), 0)
    tpos4 = jnp.concatenate([tpos] * hg, axis=0)
    lane = lax.broadcasted_iota(jnp.int32, (1, LANES), 1)

    s_cmp = _dot_nt(q4_scr[...], kc_ref[...])
    valid = (lane * CMP_STRIDE + (CMP_BLOCK - 1) <= tpos4) & (lane < n_cmp)
    p_cmp = _masked_softmax(s_cmp, valid).astype(BF16)
    o_cmp = _dot(p_cmp, vc_ref[...])

    p_heads = jnp.concatenate([p_cmp[h * tq:(h + 1) * tq] for h in range(hg)], axis=1)
    imp = _dot_nt(ovt_ref[...], p_heads)[:n_slc]
    blk = lax.broadcasted_iota(jnp.int32, (n_slc, 1), 0)
    tpos_l = t0 + lax.broadcasted_iota(jnp.int32, (1, tq), 1)
    cur = tpos_l // SEL_BLOCK
    forced = (blk == 0) | (blk == cur) | (blk == cur - 1)
    imp = jnp.where(blk * SEL_BLOCK <= tpos_l, imp + jnp.where(forced, FORCE_BONUS, 0.0), NEG_INF)
    rank = jnp.zeros((n_slc, tq), jnp.int32)
    for c in range(n_slc):
        row = imp[c:c + 1, :]
        beats = (row > imp) | ((row == imp) & (c < blk))
        rank = rank + jnp.where(beats, 1, 0)
    sel = jnp.where(rank < n_sel, 1.0, 0.0)
    sel = jnp.concatenate([sel, jnp.zeros((LANES - n_slc, tq), F32)], axis=0).T.astype(BF16)
    pick_scr[...] = _dot(sel, ex_ref[...])

    def with_ones(v):
        return jnp.concatenate([v, jnp.ones(v.shape, BF16)], axis=1)

    def normalised(acc):
        return acc[:, :HEAD_DIM] / jnp.maximum(acc[:, HEAD_DIM:], 1e-30)

    def lane_tile_max(sh):
        mx = sh[:, :LANES]
        for lt in range(1, sh.shape[1] // LANES):
            mx = jnp.maximum(mx, sh[:, lt * LANES:(lt + 1) * LANES])
        return mx

    wk = WINDOW + tq
    kw0 = pl.multiple_of(jnp.maximum(t0 - WINDOW, 0), tq)
    kpos = kw0 + lax.broadcasted_iota(jnp.int32, (1, wk), 1)
    bias = jnp.where((kpos <= tpos) & (kpos > tpos - WINDOW), 0.0, NEG_INF)
    s4 = _dot_nt(q4_scr[...], kw_ref[pl.ds(kw0, wk), :])
    v_win = with_ones(vw_ref[pl.ds(kw0, wk), :])
    for h in range(hg):
        rows = slice(h * tq, (h + 1) * tq)
        sh = s4[rows] + bias
        e = jnp.exp(sh - jnp.max(lane_tile_max(sh), axis=-1, keepdims=True)).astype(BF16)
        ow_scr[rows] = normalised(_dot(e, v_win))
    oc_scr[...] = o_cmp

    n_tiles = i // (tk // tq) + 1
    m_scr[...] = jnp.full_like(m_scr, NEG_INF)

    def scores(kt, carry):
        k0 = pl.multiple_of(kt * tk, tk)
        kpos = k0 + lax.broadcasted_iota(jnp.int32, (1, tk), 1)
        bias = jnp.where((pick_scr[:, pl.ds(k0, tk)] > 0.5) & (kpos <= tpos), 0.0, NEG_INF)
        s4 = _dot_nt(q4_scr[...], ks_ref[pl.ds(k0, tk), :])
        for h in range(hg):
            rows = slice(h * tq, (h + 1) * tq)
            sh = s4[rows] + bias
            s_scr[rows, pl.ds(k0, tk)] = sh
            m_scr[rows] = jnp.maximum(m_scr[rows], lane_tile_max(sh))
        return carry

    lax.fori_loop(0, n_tiles, scores, 0)
    m_scr[...] = jnp.broadcast_to(jnp.max(m_scr[...], axis=-1, keepdims=True), m_scr.shape)
    acc_scr[...] = jnp.zeros_like(acc_scr)

    def weighted(kt, carry):
        k0 = pl.multiple_of(kt * tk, tk)
        m = jnp.concatenate([m_scr[...]] * (tk // LANES), axis=1)
        e = jnp.exp(s_scr[:, pl.ds(k0, tk)] - m).astype(BF16)
        acc_scr[...] += _dot(e, with_ones(vs_ref[pl.ds(k0, tk), :]))
        return carry

    lax.fori_loop(0, n_tiles, weighted, 0)
    gates = gate_ref[...]
    for h in range(hg):
        rows = slice(h * tq, (h + 1) * tq)
        c0 = h * N_BRANCHES
        o = (gates[:, c0:c0 + 1] * oc_scr[rows] + gates[:, c0 + 1:c0 + 2] * normalised(acc_scr[rows])
             + gates[:, c0 + 2:c0 + 3] * ow_scr[rows])
        o_ref[:, h * HEAD_DIM:(h + 1) * HEAD_DIM] = o.astype(BF16)


def _attention(q, gates, kv_cmp, kv_rest, hg):
    bsz, s, _ = q.shape
    groups = kv_rest.shape[2]
    n_rows = kv_cmp.shape[3]
    n_cmp = n_rows - 1
    n_slc = s // SEL_BLOCK
    n_sel = min(N_SELECT, n_slc)
    tq = ATTN_TILE
    assert n_rows == LANES and n_slc <= LANES and n_slc % 8 == 0
    assert s % SLC_TILE == 0 and s >= WINDOW + tq

    ci = np.arange(LANES)[None, :]
    sj = np.arange(LANES)[:, None]
    overlap_t = ((ci * CMP_STRIDE <= sj * SEL_BLOCK + SEL_BLOCK - 1)
                 & (ci * CMP_STRIDE + CMP_BLOCK - 1 >= sj * SEL_BLOCK) & (ci < n_cmp) & (sj < n_slc))
    overlap_t = jnp.asarray(np.tile(overlap_t, (1, hg)), BF16)
    expand = jnp.asarray(np.arange(s)[None, :] // SEL_BLOCK == np.arange(LANES)[:, None], BF16)

    def kv_spec(which):
        return pl.BlockSpec((None, None, None, s, HEAD_DIM), lambda bi, gi, i: (bi, which, gi, 0, 0))

    def cmp_spec(which):
        return pl.BlockSpec((None, None, None, n_rows, HEAD_DIM), lambda bi, gi, i: (bi, which, gi, 0, 0))

    qblk = pl.BlockSpec((None, tq, hg * HEAD_DIM), lambda bi, gi, i: (bi, i, gi))
    return pl.pallas_call(
        functools.partial(_attn_kernel, hg=hg, n_cmp=n_cmp, n_slc=n_slc, n_sel=n_sel),
        grid=(bsz, groups, s // tq),
        in_specs=[
            qblk,
            pl.BlockSpec((None, tq, LANES), lambda bi, gi, i: (bi, i, gi)),
            cmp_spec(0), cmp_spec(1),
            kv_spec(0), kv_spec(1), kv_spec(2), kv_spec(3),
            pl.BlockSpec((LANES, hg * LANES), lambda bi, gi, i: (0, 0)),
            pl.BlockSpec((LANES, s), lambda bi, gi, i: (0, 0)),
        ],
        out_specs=qblk,
        out_shape=jax.ShapeDtypeStruct(q.shape, BF16),
        scratch_shapes=[pltpu.VMEM((hg * tq, HEAD_DIM), BF16),
                        pltpu.VMEM((tq, s), F32),
                        pltpu.VMEM((hg * tq, s), F32),
                        pltpu.VMEM((hg * tq, LANES), F32),
                        pltpu.VMEM((hg * tq, 2 * HEAD_DIM), F32),
                        pltpu.VMEM((hg * tq, HEAD_DIM), F32),
                        pltpu.VMEM((hg * tq, HEAD_DIM), F32)],
        compiler_params=_params(("parallel", "parallel", "arbitrary")),
        name="nsa_attention",
    )(q, gates, kv_cmp, kv_cmp, kv_rest, kv_rest, kv_rest, kv_rest, overlap_t, expand)


def _rope_tables(s):
    freqs = ROPE_THETA ** (-jnp.arange(HALF_HEAD, dtype=F32) / HALF_HEAD)
    ang = jnp.arange(s).astype(F32)[:, None] * freqs[None, :]
    cos, sin = jnp.cos(ang), jnp.sin(ang)
    return jnp.concatenate([cos, cos], axis=1), jnp.concatenate([-sin, sin], axis=1)


def kernel(x, c, mod_w, mod_b, norm_g, mlp_w1, mlp_w2, a_w_in, a_b_in, a_ln_g, a_ln_b, a_w_s, a_b_s, a_w_out, a_b_out, kv_norm_g, kv_mod_w, kv_mod_b, w_kv, cmp_pos_k, cmp_w1_k, cmp_w2_k, cmp_pos_v, cmp_w1_v, cmp_w2_v, b_w_qg, b_w_o, final_g):
    bsz, s, d = x.shape
    depth = mod_w.shape[0]
    n_a = a_w_in.shape[0]
    groups = w_kv.shape[1] // (2 * N_BRANCHES * HEAD_DIM)
    heads = b_w_o.shape[1] // HEAD_DIM
    hg = heads // groups
    assert hg * N_BRANCHES <= LANES

    mod = _cond_matmul(c, mod_w.reshape(depth * 2, d, 3 * d), mod_b.reshape(depth * 2, 3 * d))
    mod = mod.reshape(depth, 2, bsz, 3, 1, d)
    kv_mod = _cond_matmul(c, kv_mod_w[None], kv_mod_b[None]).reshape(bsz, 2, 1, d)
    cos, sin = _rope_tables(s)

    for layer in range(depth):
        shift, scale, gate = (mod[layer, 0, :, t] for t in range(3))
        if layer < n_a:
            u, v = _gmlp_in(x, norm_g[layer, 0], shift, scale,
                            a_w_in[layer].astype(BF16), a_b_in[layer])
            mixed = _gmlp_gate(u, v, a_ln_g[layer], a_ln_b[layer], a_w_s[layer], a_b_s[layer])
            x = _mm_residual(mixed, a_w_out[layer].astype(BF16), a_b_out[layer], x, gate)
        else:
            j = layer - n_a
            if layer == n_a:
                kv_cmp, kv_rest = _kv_proj(x, kv_norm_g, kv_mod[:, 0], kv_mod[:, 1],
                                           w_kv.astype(BF16), cos, sin, groups)
                pos = jnp.stack([cmp_pos_k, cmp_pos_v]).reshape(2, 2, CMP_STRIDE * HEAD_DIM)
                kv_cmp = _compress(kv_cmp, pos,
                                   jnp.stack([cmp_w1_k, cmp_w1_v]).astype(BF16),
                                   jnp.stack([cmp_w2_k, cmp_w2_v]).astype(BF16))
            w_qg = b_w_qg[j]
            w_q = w_qg[:, :heads * HEAD_DIM]
            w_gate = w_qg[:, heads * HEAD_DIM:].reshape(d, groups, hg * N_BRANCHES)
            w_gate = jnp.pad(w_gate, ((0, 0), (0, 0), (0, LANES - hg * N_BRANCHES)))
            w_gate = w_gate.reshape(d, groups * LANES)
            tn = hg * HEAD_DIM
            pad = (-w_gate.shape[1]) % tn
            w_gate = jnp.pad(w_gate, ((0, 0), (0, pad)))
            q, gates = _qg_proj(x, norm_g[layer, 0], shift, scale,
                                w_q.astype(BF16), w_gate.astype(BF16), cos, sin, tn)
            o = _attention(q, gates, kv_cmp, kv_rest, hg)
            x = _mm_residual(o, b_w_o[j].astype(BF16), jnp.zeros((d,), F32), x, gate)

        shift, scale, gate = (mod[layer, 1, :, t] for t in range(3))
        x = _mlp(x, norm_g[layer, 1], shift, scale, gate,
                 mlp_w1[layer].astype(BF16), mlp_w2[layer].astype(BF16),
                 final_g, final_norm=(layer == depth - 1))
    return x
```

```python
import functools

import numpy as np
import jax
import jax.numpy as jnp
from jax import lax
from jax.experimental import pallas as pl
from jax.experimental.pallas import tpu as pltpu

BF16 = jnp.bfloat16
F32 = jnp.float32

LANES = 128
SUBLANES = 8
MXU_COLS = 256
HEAD_DIM = 128
HALF_HEAD = HEAD_DIM // 2
N_BRANCHES = 3
CMP_BLOCK = 32
CMP_STRIDE = 16
SEL_BLOCK = 64
N_SELECT = 8
WINDOW = 512
ROPE_THETA = 10000.0
EPS = 1e-6
NEG_INF = -1e30
FORCE_BONUS = 1e6
ATTN_TILE = 256
SLC_TILE = 512
VMEM_LIMIT = 56 * 1024 * 1024


def _dot(a, b):
    return jnp.dot(a, b, preferred_element_type=F32)


def _dot_nt(a, b):
    return lax.dot_general(a, b, (((1,), (1,)), ((), ())), preferred_element_type=F32)


def _tile(n, pref):
    return pref if n % pref == 0 else n


def _params(semantics):
    return pltpu.CompilerParams(dimension_semantics=semantics, vmem_limit_bytes=VMEM_LIMIT)


def _rms(x, g):
    return (x * lax.rsqrt(jnp.mean(x * x, axis=-1, keepdims=True) + EPS)) * g


def _norm_mod(x, g, shift, scale):
    return _rms(x, g) * (1.0 + scale) + shift


def _rope(a, cos, sin):
    return a * cos + pltpu.roll(a, HALF_HEAD, 1) * sin


def _dot_chunks(h_ref, w_ref, epilogue):
    n = w_ref.shape[1]
    chunk = MXU_COLS if n % MXU_COLS == 0 else n
    for c0 in range(0, n, chunk):
        cols = slice(c0, c0 + chunk)
        epilogue(cols, _dot(h_ref[...], w_ref[:, cols]))


def _cond_kernel(c_ref, w_ref, b_ref, o_ref):
    cond = jax.nn.silu(c_ref[...]).astype(BF16)
    o_ref[...] = _dot(cond, w_ref[...].astype(BF16)) + b_ref[...]


def _cond_matmul(c, w, b):
    n, d, e = w.shape
    bsz = c.shape[0]
    te = _tile(e, 1024)
    return pl.pallas_call(
        _cond_kernel,
        grid=(n, e // te),
        in_specs=[
            pl.BlockSpec((bsz, d), lambda s, j: (0, 0)),
            pl.BlockSpec((None, d, te), lambda s, j: (s, 0, j)),
            pl.BlockSpec((None, 1, te), lambda s, j: (s, 0, j)),
        ],
        out_specs=pl.BlockSpec((None, bsz, te), lambda s, j: (s, 0, j)),
        out_shape=jax.ShapeDtypeStruct((n, bsz, e), F32),
        compiler_params=_params(("parallel", "parallel")),
        name="cond_matmul",
    )(c, w, b.reshape(n, 1, e))


def _gmlp_in_kernel(x_ref, g_ref, sh_ref, sc_ref, w_ref, b_ref, u_ref, v_ref, h_scr, *, n_u):
    j = pl.program_id(2)

    @pl.when(j == 0)
    def _():
        h_scr[...] = _norm_mod(x_ref[...], g_ref[...], sh_ref[...], sc_ref[...]).astype(BF16)

    def emit(dst):
        def store(cols, acc):
            dst[:, cols] = jax.nn.gelu(acc + b_ref[:, cols])
        _dot_chunks(h_scr, w_ref, store)

    @pl.when(j < n_u)
    def _():
        emit(u_ref)

    @pl.when(j >= n_u)
    def _():
        emit(v_ref)


def _gmlp_in(x, g, shift, scale, w, b):
    bsz, s, d = x.shape
    n = w.shape[1]
    width = n // 2
    tm = _tile(s, 1024)
    tn = _tile(width, 1024)
    n_u = width // tn
    row = lambda bi, i, j: (bi, 0, 0)
    return pl.pallas_call(
        functools.partial(_gmlp_in_kernel, n_u=n_u),
        grid=(bsz, s // tm, n // tn),
        in_specs=[
            pl.BlockSpec((None, tm, d), lambda bi, i, j: (bi, i, 0)),
            pl.BlockSpec((1, d), lambda bi, i, j: (0, 0)),
            pl.BlockSpec((None, 1, d), row),
            pl.BlockSpec((None, 1, d), row),
            pl.BlockSpec((d, tn), lambda bi, i, j: (0, j)),
            pl.BlockSpec((1, tn), lambda bi, i, j: (0, j)),
        ],
        out_specs=[
            pl.BlockSpec((None, tm, tn), lambda bi, i, j: (bi, i, jnp.minimum(j, n_u - 1))),
            pl.BlockSpec((None, tm, tn), lambda bi, i, j: (bi, i, jnp.maximum(j - n_u, 0))),
        ],
        out_shape=[jax.ShapeDtypeStruct((bsz, s, width), F32)] * 2,
        scratch_shapes=[pltpu.VMEM((tm, d), BF16)],
        compiler_params=_params(("parallel", "parallel", "arbitrary")),
        name="gmlp_in",
    )(x, g.reshape(1, d), shift, scale, w, b.reshape(1, n))


def _gmlp_gate_kernel(u_ref, v_ref, lg_ref, lb_ref, ws_ref, bs_ref, o_ref, *, groups, gdim):
    v = v_ref[...]
    mu = jnp.mean(v, axis=-1, keepdims=True)
    dv = v - mu
    var = jnp.mean(dv * dv, axis=-1, keepdims=True)
    vn = ((dv * lax.rsqrt(var + EPS)) * lg_ref[...] + lb_ref[...]).astype(BF16)
    chunk = v.shape[0]
    causal = (lax.broadcasted_iota(jnp.int32, (chunk, chunk), 1)
              <= lax.broadcasted_iota(jnp.int32, (chunk, chunk), 0))
    for gi in range(groups):
        cols = slice(gi * gdim, (gi + 1) * gdim)
        w = jnp.where(causal, ws_ref[gi], 0.0).astype(BF16)
        mixed = _dot(w, vn[:, cols]) + bs_ref[:, gi:gi + 1]
        o_ref[:, cols] = (u_ref[:, cols] * mixed).astype(BF16)


def _gmlp_gate(u, v, ln_g, ln_b, w_s, b_s):
    bsz, s, width = u.shape
    groups, chunk, _ = w_s.shape
    gdim = width // groups
    blk = pl.BlockSpec((None, chunk, width), lambda bi, i: (bi, i, 0))
    vec = pl.BlockSpec((1, width), lambda bi, i: (0, 0))
    return pl.pallas_call(
        functools.partial(_gmlp_gate_kernel, groups=groups, gdim=gdim),
        grid=(bsz, s // chunk),
        in_specs=[blk, blk, vec, vec,
                  pl.BlockSpec((groups, chunk, chunk), lambda bi, i: (0, 0, 0)),
                  pl.BlockSpec((chunk, groups), lambda bi, i: (0, 0))],
        out_specs=blk,
        out_shape=jax.ShapeDtypeStruct((bsz, s, width), BF16),
        compiler_params=_params(("parallel", "parallel")),
        name="gmlp_gate",
    )(u, v, ln_g.reshape(1, width), ln_b.reshape(1, width), w_s, b_s.T)


def _mm_res_kernel(a_ref, w_ref, b_ref, x_ref, gate_ref, o_ref):
    y = _dot(a_ref[...], w_ref[...]) + b_ref[...]
    o_ref[...] = x_ref[...] + gate_ref[...] * y


def _mm_residual(a, w, b, x, gate):
    bsz, s, k = a.shape
    d = w.shape[1]
    tm = _tile(s, 1024)
    tn = _tile(d, 512)
    return pl.pallas_call(
        _mm_res_kernel,
        grid=(bsz, s // tm, d // tn),
        in_specs=[
            pl.BlockSpec((None, tm, k), lambda bi, i, j: (bi, i, 0)),
            pl.BlockSpec((k, tn), lambda bi, i, j: (0, j)),
            pl.BlockSpec((1, tn), lambda bi, i, j: (0, j)),
            pl.BlockSpec((None, tm, tn), lambda bi, i, j: (bi, i, j)),
            pl.BlockSpec((None, 1, tn), lambda bi, i, j: (bi, 0, j)),
        ],
        out_specs=pl.BlockSpec((None, tm, tn), lambda bi, i, j: (bi, i, j)),
        out_shape=jax.ShapeDtypeStruct((bsz, s, d), F32),
        compiler_params=_params(("parallel", "parallel", "parallel")),
        name="matmul_residual",
    )(a, w, b.reshape(1, d), x, gate)


def _mlp_kernel(x_ref, g_ref, sh_ref, sc_ref, gate_ref, w1_ref, w2_ref, fg_ref, o_ref,
                h_scr, acc_scr, *, final_norm):
    f = pl.program_id(2)

    @pl.when(f == 0)
    def _():
        h_scr[...] = _norm_mod(x_ref[...], g_ref[...], sh_ref[...], sc_ref[...]).astype(BF16)
        acc_scr[...] = jnp.zeros_like(acc_scr)

    a = jnp.square(jnp.maximum(_dot(h_scr[...], w1_ref[...]), 0.0)).astype(BF16)
    acc_scr[...] += _dot(a, w2_ref[...])

    @pl.when(f == pl.num_programs(2) - 1)
    def _():
        xn = x_ref[...] + gate_ref[...] * acc_scr[...]
        if final_norm:
            xn = _rms(xn, fg_ref[...])
        o_ref[...] = xn


def _mlp(x, g, shift, scale, gate, w1, w2, final_g, final_norm):
    bsz, s, d = x.shape
    dff = w1.shape[1]
    tm = _tile(s, 512)
    tf = _tile(dff, 1024)
    row = lambda bi, i, f: (bi, 0, 0)
    vec = pl.BlockSpec((1, d), lambda bi, i, f: (0, 0))
    xblk = pl.BlockSpec((None, tm, d), lambda bi, i, f: (bi, i, 0))
    return pl.pallas_call(
        functools.partial(_mlp_kernel, final_norm=final_norm),
        grid=(bsz, s // tm, dff // tf),
        in_specs=[
            xblk, vec,
            pl.BlockSpec((None, 1, d), row), pl.BlockSpec((None, 1, d), row),
            pl.BlockSpec((None, 1, d), row),
            pl.BlockSpec((d, tf), lambda bi, i, f: (0, f)),
            pl.BlockSpec((tf, d), lambda bi, i, f: (f, 0)),
            vec,
        ],
        out_specs=xblk,
        out_shape=jax.ShapeDtypeStruct((bsz, s, d), F32),
        scratch_shapes=[pltpu.VMEM((tm, d), BF16), pltpu.VMEM((tm, d), F32)],
        compiler_params=_params(("parallel", "parallel", "arbitrary")),
        name="mlp",
    )(x, g.reshape(1, d), shift, scale, gate, w1, w2, final_g.reshape(1, d))


def _kv_kernel(x_ref, g_ref, sh_ref, sc_ref, w_ref, cos_ref, sin_ref, cmp_ref, rest_ref, h_scr,
               *, groups):
    j = pl.program_id(2)

    @pl.when(j == 0)
    def _():
        h_scr[...] = _norm_mod(x_ref[...], g_ref[...], sh_ref[...], sc_ref[...]).astype(BF16)

    def emit(dst, rotate):
        def store(cols, acc):
            for c0 in range(0, acc.shape[1], HEAD_DIM):
                a = acc[:, c0:c0 + HEAD_DIM]
                if rotate:
                    a = _rope(a, cos_ref[...], sin_ref[...])
                dst[(cols.start + c0) // HEAD_DIM] = a.astype(dst.dtype)
        _dot_chunks(h_scr, w_ref, store)

    @pl.when(j == 0)
    def _():
        emit(cmp_ref, True)

    @pl.when(j == 1)
    def _():
        emit(cmp_ref, False)

    @pl.when((j >= 2) & (j % 2 == 0))
    def _():
        emit(rest_ref, True)

    @pl.when((j >= 2) & (j % 2 == 1))
    def _():
        emit(rest_ref, False)


def _kv_proj(x, g, shift, scale, w, cos, sin, groups):
    bsz, s, d = x.shape
    tm = _tile(s, 1024)
    tn = groups * HEAD_DIM
    row = lambda bi, i, j: (bi, 0, 0)
    tab = pl.BlockSpec((tm, HEAD_DIM), lambda bi, i, j: (i, 0))
    return pl.pallas_call(
        functools.partial(_kv_kernel, groups=groups),
        grid=(bsz, s // tm, 2 * N_BRANCHES),
        in_specs=[
            pl.BlockSpec((None, tm, d), lambda bi, i, j: (bi, i, 0)),
            pl.BlockSpec((1, d), lambda bi, i, j: (0, 0)),
            pl.BlockSpec((None, 1, d), row), pl.BlockSpec((None, 1, d), row),
            pl.BlockSpec((d, tn), lambda bi, i, j: (0, j)),
            tab, tab,
        ],
        out_specs=[
            pl.BlockSpec((None, None, groups, tm, HEAD_DIM),
                         lambda bi, i, j: (bi, jnp.minimum(j, 1), 0, i, 0)),
            pl.BlockSpec((None, None, groups, tm, HEAD_DIM),
                         lambda bi, i, j: (bi, jnp.maximum(j - 2, 0), 0, i, 0)),
        ],
        out_shape=[jax.ShapeDtypeStruct((bsz, 2, groups, s, HEAD_DIM), F32),
                   jax.ShapeDtypeStruct((bsz, 4, groups, s, HEAD_DIM), BF16)],
        scratch_shapes=[pltpu.VMEM((tm, d), BF16)],
        compiler_params=_params(("parallel", "parallel", "arbitrary")),
        name="kv_proj",
    )(x, g.reshape(1, d), shift, scale, w, cos, sin)


def _qg_kernel(x_ref, g_ref, sh_ref, sc_ref, w_ref, cos_ref, sin_ref, q_ref, gate_ref, h_scr,
               *, heads_per_tile, n_q):
    j = pl.program_id(2)

    @pl.when(j == 0)
    def _():
        h_scr[...] = _norm_mod(x_ref[...], g_ref[...], sh_ref[...], sc_ref[...]).astype(BF16)

    @pl.when(j < n_q)
    def _():
        def store(cols, acc):
            for c0 in range(0, acc.shape[1], HEAD_DIM):
                a = _rope(acc[:, c0:c0 + HEAD_DIM], cos_ref[...], sin_ref[...]) * (HEAD_DIM ** -0.5)
                q_ref[:, cols.start + c0:cols.start + c0 + HEAD_DIM] = a.astype(BF16)
        _dot_chunks(h_scr, w_ref, store)

    @pl.when(j >= n_q)
    def _():
        def store(cols, acc):
            gate_ref[:, cols] = jax.nn.sigmoid(acc)
        _dot_chunks(h_scr, w_ref, store)


def _qg_proj(x, g, shift, scale, w_q, w_gate, cos, sin, tn):
    bsz, s, d = x.shape
    nq_cols, ng_cols = w_q.shape[1], w_gate.shape[1]
    n_q, n_g = nq_cols // tn, ng_cols // tn
    tm = _tile(s, 1024)
    row = lambda bi, i, j: (bi, 0, 0)
    tab = pl.BlockSpec((tm, HEAD_DIM), lambda bi, i, j: (i, 0))
    w = jnp.concatenate([w_q, w_gate], axis=1)
    return pl.pallas_call(
        functools.partial(_qg_kernel, heads_per_tile=tn // HEAD_DIM, n_q=n_q),
        grid=(bsz, s // tm, n_q + n_g),
        in_specs=[
            pl.BlockSpec((None, tm, d), lambda bi, i, j: (bi, i, 0)),
            pl.BlockSpec((1, d), lambda bi, i, j: (0, 0)),
            pl.BlockSpec((None, 1, d), row), pl.BlockSpec((None, 1, d), row),
            pl.BlockSpec((d, tn), lambda bi, i, j: (0, j)),
            tab, tab,
        ],
        out_specs=[
            pl.BlockSpec((None, tm, tn), lambda bi, i, j: (bi, i, jnp.minimum(j, n_q - 1))),
            pl.BlockSpec((None, tm, tn), lambda bi, i, j: (bi, i, jnp.maximum(j - n_q, 0))),
        ],
        out_shape=[jax.ShapeDtypeStruct((bsz, s, nq_cols), BF16),
                   jax.ShapeDtypeStruct((bsz, s, ng_cols), F32)],
        scratch_shapes=[pltpu.VMEM((tm, d), BF16)],
        compiler_params=_params(("parallel", "parallel", "arbitrary")),
        name="qg_proj",
    )(x, g.reshape(1, d), shift, scale, w, cos, sin)


def _compress_kernel(h_ref, pos_ref, w1_ref, w2_ref, w2t_ref, o_ref, ot_ref):
    hv = h_ref[...]
    half = hv.shape[1]
    lo = (hv + pos_ref[0:1, :]).astype(BF16)
    hi = (hv + pos_ref[1:2, :]).astype(BF16)
    first = _dot(lo, w1_ref[0:half, :])
    second = _dot(hi, w1_ref[half:2 * half, :])
    n_rows = hv.shape[0]
    pre = first + pltpu.roll(second, n_rows - 1, 0)
    hidden = jax.nn.gelu(pre).astype(BF16)
    o_ref[...] = _dot(hidden, w2_ref[...]).astype(BF16)
    ot_ref[...] = _dot_nt(w2t_ref[...], hidden).astype(BF16)


def _compress(kv_cmp, pos, w1, w2):
    bsz, _, groups, s, _ = kv_cmp.shape
    n_rows = s // CMP_STRIDE
    flat = CMP_STRIDE * HEAD_DIM
    hidden = w1.shape[2]
    h = kv_cmp.reshape(bsz, 2, groups, n_rows, flat)
    return pl.pallas_call(
        _compress_kernel,
        grid=(bsz, 2, groups),
        in_specs=[
            pl.BlockSpec((None, None, None, n_rows, flat), lambda bi, t, gi: (bi, t, gi, 0, 0)),
            pl.BlockSpec((None, 2, flat), lambda bi, t, gi: (t, 0, 0)),
            pl.BlockSpec((None, 2 * flat, hidden), lambda bi, t, gi: (t, 0, 0)),
            pl.BlockSpec((None, hidden, HEAD_DIM), lambda bi, t, gi: (t, 0, 0)),
            pl.BlockSpec((None, HEAD_DIM, hidden), lambda bi, t, gi: (t, 0, 0)),
        ],
        out_specs=[pl.BlockSpec((None, None, None, n_rows, HEAD_DIM), lambda bi, t, gi: (bi, t, gi, 0, 0)),
                   pl.BlockSpec((None, None, None, HEAD_DIM, n_rows), lambda bi, t, gi: (bi, t, gi, 0, 0))],
        out_shape=[jax.ShapeDtypeStruct((bsz, 2, groups, n_rows, HEAD_DIM), BF16),
                   jax.ShapeDtypeStruct((bsz, 2, groups, HEAD_DIM, n_rows), BF16)],
        compiler_params=_params(("parallel", "parallel", "parallel")),
        name="compress",
    )(h, pos, w1, w2, jnp.swapaxes(w2, 1, 2))


def _masked_softmax(s, mask, axis):
    s = jnp.where(mask, s, NEG_INF)
    m = jnp.max(s, axis=axis, keepdims=True)
    e = jnp.where(mask, jnp.exp(s - m), 0.0)
    den = jnp.sum(e, axis=axis, keepdims=True)
    return e / jnp.maximum(den, 1e-30)


def _attn_kernel(q_ref, gate_ref, kc_ref, vct_ref, ks_ref, vst_ref, kw_ref, vwt_ref, ovt_ref,
                 o_ref, q4_scr, sel_scr, s_scr, m_scr, acc_scr, oc_scr, ow_scr,
                 *, hg, n_cmp, n_slc, n_sel):
    tq = ATTN_TILE
    tk = SLC_TILE
    i = pl.program_id(2)
    t0 = i * tq
    for h in range(hg):
        q4_scr[h * tq:(h + 1) * tq, :] = q_ref[:, h * HEAD_DIM:(h + 1) * HEAD_DIM]
    q4 = q4_scr[...]
    tpos_l = t0 + lax.broadcasted_iota(jnp.int32, (1, tq), 1)
    tpos4_l = jnp.concatenate([tpos_l] * hg, axis=1)

    def heads(a, h):
        return a[:, h * tq:(h + 1) * tq]

    n_idx = lax.broadcasted_iota(jnp.int32, (kc_ref.shape[0], 1), 0)
    valid = (n_idx * CMP_STRIDE + (CMP_BLOCK - 1) <= tpos4_l) & (n_idx < n_cmp)
    p_cmp = _masked_softmax(_dot_nt(kc_ref[...], q4), valid, 0).astype(BF16)
    oc_scr[...] = _dot(vct_ref[...], p_cmp)

    p_heads = jnp.concatenate([heads(p_cmp, h) for h in range(hg)], axis=0)
    imp = _dot(ovt_ref[...], p_heads)[:n_slc]
    blk = lax.broadcasted_iota(jnp.int32, (n_slc, 1), 0)
    cur = tpos_l // SEL_BLOCK
    forced = (blk == 0) | (blk == cur) | (blk == cur - 1)
    imp = jnp.where(blk * SEL_BLOCK <= tpos_l, imp + jnp.where(forced, FORCE_BONUS, 0.0), NEG_INF)
    rank = jnp.zeros((n_slc, tq), jnp.int32)
    for c in range(n_slc):
        row = imp[c:c + 1, :]
        beats = (row > imp) | ((row == imp) & (c < blk))
        rank = rank + jnp.where(beats, 1, 0)
    sel_scr[...] = jnp.where(rank < n_sel, 1.0, 0.0)

    def with_ones(vt):
        return jnp.concatenate([vt, jnp.ones(vt.shape, BF16)], axis=0)

    def normalised(acc):
        return acc[:HEAD_DIM] / jnp.maximum(acc[HEAD_DIM:], 1e-30)

    def sublane_tile_max(sh):
        return jnp.max(sh.reshape(sh.shape[0] // SUBLANES, SUBLANES, sh.shape[1]), axis=0)

    wk = WINDOW + tq
    kw0 = pl.multiple_of(jnp.maximum(t0 - WINDOW, 0), tq)
    kpos = kw0 + lax.broadcasted_iota(jnp.int32, (wk, 1), 0)
    bias = jnp.where((kpos <= tpos_l) & (kpos > tpos_l - WINDOW), 0.0, NEG_INF)
    s4 = _dot_nt(kw_ref[pl.ds(kw0, wk), :], q4)
    e = []
    for h in range(hg):
        sh = heads(s4, h) + bias
        m = jnp.max(sublane_tile_max(sh), axis=0, keepdims=True)
        e.append(jnp.exp(sh - m).astype(BF16))
    ow_scr[...] = normalised(_dot(with_ones(vwt_ref[:, pl.ds(kw0, wk)]), jnp.concatenate(e, axis=1)))

    n_tiles = i // (tk // tq) + 1
    blocks_per_tile = tk // SEL_BLOCK
    m_scr[...] = jnp.full_like(m_scr, NEG_INF)

    def scores(kt, carry):
        k0 = pl.multiple_of(kt * tk, tk)
        sel = sel_scr[pl.ds(pl.multiple_of(kt * blocks_per_tile, blocks_per_tile), blocks_per_tile), :]
        bias = []
        for jb in range(blocks_per_tile):
            kpos = k0 + jb * SEL_BLOCK + lax.broadcasted_iota(jnp.int32, (SEL_BLOCK, 1), 0)
            bias.append(jnp.where((sel[jb:jb + 1, :] > 0.5) & (kpos <= tpos_l), 0.0, NEG_INF))
        bias = jnp.concatenate(bias, axis=0)
        s4 = _dot_nt(ks_ref[pl.ds(k0, tk), :], q4)
        for h in range(hg):
            cols = slice(h * tq, (h + 1) * tq)
            sh = heads(s4, h) + bias
            s_scr[pl.ds(k0, tk), cols] = sh
            m_scr[:, cols] = jnp.maximum(m_scr[:, cols], sublane_tile_max(sh))
        return carry

    lax.fori_loop(0, n_tiles, scores, 0)
    m_scr[...] = jnp.broadcast_to(jnp.max(m_scr[...], axis=0, keepdims=True), m_scr.shape)
    acc_scr[...] = jnp.zeros_like(acc_scr)

    def weighted(kt, carry):
        k0 = pl.multiple_of(kt * tk, tk)
        s = s_scr[pl.ds(k0, tk), :].reshape(tk // SUBLANES, SUBLANES, hg * tq)
        e = jnp.exp(s - m_scr[...]).reshape(tk, hg * tq).astype(BF16)
        acc_scr[...] += _dot(with_ones(vst_ref[:, pl.ds(k0, tk)]), e)
        return carry

    lax.fori_loop(0, n_tiles, weighted, 0)

    o_slc = normalised(acc_scr[...])
    gates_t = gate_ref[...].T
    for h in range(hg):
        c0 = h * N_BRANCHES
        o = (gates_t[c0:c0 + 1] * heads(oc_scr, h) + gates_t[c0 + 1:c0 + 2] * heads(o_slc, h)
             + gates_t[c0 + 2:c0 + 3] * heads(ow_scr, h))
        o_ref[:, h * HEAD_DIM:(h + 1) * HEAD_DIM] = o.T.astype(BF16)


def _attention(q, gates, kv_cmp, kv_cmp_t, kv_rest, v_rest_t, hg):
    bsz, s, _ = q.shape
    groups = kv_rest.shape[2]
    n_rows = kv_cmp.shape[3]
    n_cmp = n_rows - 1
    n_slc = s // SEL_BLOCK
    n_sel = min(N_SELECT, n_slc)
    tq = ATTN_TILE
    assert n_rows == LANES and n_slc <= LANES and n_slc % (SLC_TILE // SEL_BLOCK) == 0
    assert s % SLC_TILE == 0 and s >= WINDOW + tq

    ci = np.arange(LANES)[None, :]
    sj = np.arange(LANES)[:, None]
    overlap_t = ((ci * CMP_STRIDE <= sj * SEL_BLOCK + SEL_BLOCK - 1)
                 & (ci * CMP_STRIDE + CMP_BLOCK - 1 >= sj * SEL_BLOCK) & (ci < n_cmp) & (sj < n_slc))
    overlap_t = jnp.asarray(np.tile(overlap_t, (1, hg)), BF16)

    def k_spec(which):
        return pl.BlockSpec((None, None, None, s, HEAD_DIM), lambda bi, gi, i: (bi, which, gi, 0, 0))

    def vt_spec(which):
        return pl.BlockSpec((None, None, None, HEAD_DIM, s), lambda bi, gi, i: (bi, which, gi, 0, 0))

    def cmp_spec(which):
        return pl.BlockSpec((None, None, None, n_rows, HEAD_DIM), lambda bi, gi, i: (bi, which, gi, 0, 0))

    qblk = pl.BlockSpec((None, tq, hg * HEAD_DIM), lambda bi, gi, i: (bi, i, gi))
    return pl.pallas_call(
        functools.partial(_attn_kernel, hg=hg, n_cmp=n_cmp, n_slc=n_slc, n_sel=n_sel),
        grid=(bsz, groups, s // tq),
        in_specs=[
            qblk,
            pl.BlockSpec((None, tq, LANES), lambda bi, gi, i: (bi, i, gi)),
            cmp_spec(0), cmp_spec(1),
            k_spec(0), vt_spec(0), k_spec(2), vt_spec(1),
            pl.BlockSpec((LANES, hg * LANES), lambda bi, gi, i: (0, 0)),
        ],
        out_specs=qblk,
        out_shape=jax.ShapeDtypeStruct(q.shape, BF16),
        scratch_shapes=[pltpu.VMEM((hg * tq, HEAD_DIM), BF16),
                        pltpu.VMEM((n_slc, tq), F32),
                        pltpu.VMEM((s, hg * tq), F32),
                        pltpu.VMEM((SUBLANES, hg * tq), F32),
                        pltpu.VMEM((2 * HEAD_DIM, hg * tq), F32),
                        pltpu.VMEM((HEAD_DIM, hg * tq), F32),
                        pltpu.VMEM((HEAD_DIM, hg * tq), F32)],
        compiler_params=_params(("parallel", "parallel", "arbitrary")),
        name="nsa_attention",
    )(q, gates, kv_cmp, kv_cmp_t, kv_rest, v_rest_t, kv_rest, v_rest_t, overlap_t)


def _rope_tables(s):
    freqs = ROPE_THETA ** (-jnp.arange(HALF_HEAD, dtype=F32) / HALF_HEAD)
    ang = jnp.arange(s).astype(F32)[:, None] * freqs[None, :]
    cos, sin = jnp.cos(ang), jnp.sin(ang)
    return jnp.concatenate([cos, cos], axis=1), jnp.concatenate([-sin, sin], axis=1)


def kernel(x, c, mod_w, mod_b, norm_g, mlp_w1, mlp_w2, a_w_in, a_b_in, a_ln_g, a_ln_b, a_w_s, a_b_s, a_w_out, a_b_out, kv_norm_g, kv_mod_w, kv_mod_b, w_kv, cmp_pos_k, cmp_w1_k, cmp_w2_k, cmp_pos_v, cmp_w1_v, cmp_w2_v, b_w_qg, b_w_o, final_g):
    bsz, s, d = x.shape
    depth = mod_w.shape[0]
    n_a = a_w_in.shape[0]
    groups = w_kv.shape[1] // (2 * N_BRANCHES * HEAD_DIM)
    heads = b_w_o.shape[1] // HEAD_DIM
    hg = heads // groups
    assert hg * N_BRANCHES <= LANES

    mod = _cond_matmul(c, mod_w.reshape(depth * 2, d, 3 * d), mod_b.reshape(depth * 2, 3 * d))
    mod = mod.reshape(depth, 2, bsz, 3, 1, d)
    kv_mod = _cond_matmul(c, kv_mod_w[None], kv_mod_b[None]).reshape(bsz, 2, 1, d)
    cos, sin = _rope_tables(s)

    for layer in range(depth):
        shift, scale, gate = (mod[layer, 0, :, t] for t in range(3))
        if layer < n_a:
            u, v = _gmlp_in(x, norm_g[layer, 0], shift, scale,
                            a_w_in[layer].astype(BF16), a_b_in[layer])
            mixed = _gmlp_gate(u, v, a_ln_g[layer], a_ln_b[layer], a_w_s[layer], a_b_s[layer])
            x = _mm_residual(mixed, a_w_out[layer].astype(BF16), a_b_out[layer], x, gate)
        else:
            j = layer - n_a
            if layer == n_a:
                kv_cmp, kv_rest = _kv_proj(x, kv_norm_g, kv_mod[:, 0], kv_mod[:, 1],
                                           w_kv.astype(BF16), cos, sin, groups)
                pos = jnp.stack([cmp_pos_k, cmp_pos_v]).reshape(2, 2, CMP_STRIDE * HEAD_DIM)
                kv_cmp, kv_cmp_t = _compress(kv_cmp, pos,
                                             jnp.stack([cmp_w1_k, cmp_w1_v]).astype(BF16),
                                             jnp.stack([cmp_w2_k, cmp_w2_v]).astype(BF16))
                v_rest_t = jnp.swapaxes(kv_rest[:, 1::2], 3, 4)
            w_qg = b_w_qg[j]
            w_q = w_qg[:, :heads * HEAD_DIM]
            w_gate = w_qg[:, heads * HEAD_DIM:].reshape(d, groups, hg * N_BRANCHES)
            w_gate = jnp.pad(w_gate, ((0, 0), (0, 0), (0, LANES - hg * N_BRANCHES)))
            w_gate = w_gate.reshape(d, groups * LANES)
            tn = hg * HEAD_DIM
            pad = (-w_gate.shape[1]) % tn
            w_gate = jnp.pad(w_gate, ((0, 0), (0, pad)))
            q, gates = _qg_proj(x, norm_g[layer, 0], shift, scale,
                                w_q.astype(BF16), w_gate.astype(BF16), cos, sin, tn)
            o = _attention(q, gates, kv_cmp, kv_cmp_t, kv_rest, v_rest_t, hg)
            x = _mm_residual(o, b_w_o[j].astype(BF16), jnp.zeros((d,), F32), x, gate)

        shift, scale, gate = (mod[layer, 1, :, t] for t in range(3))
        x = _mlp(x, norm_g[layer, 1], shift, scale, gate,
                 mlp_w1[layer].astype(BF16), mlp_w2[layer].astype(BF16),
                 final_g, final_norm=(layer == depth - 1))
    return x
```

```python
import functools

import numpy as np
import jax
import jax.numpy as jnp
from jax import lax
from jax.experimental import pallas as pl
from jax.experimental.pallas import tpu as pltpu

BF16 = jnp.bfloat16
F32 = jnp.float32

LANES = 128
SUBLANES = 8
MXU_COLS = 256
HEAD_DIM = 128
HALF_HEAD = HEAD_DIM // 2
N_BRANCHES = 3
CMP_BLOCK = 32
CMP_STRIDE = 16
SEL_BLOCK = 64
N_SELECT = 8
WINDOW = 512
ROPE_THETA = 10000.0
EPS = 1e-6
NEG_INF = -1e30
FORCE_BONUS = 1e6
ATTN_TILE = 256
SLC_TILE = 512
VMEM_LIMIT = 56 * 1024 * 1024


def _dot(a, b):
    return jnp.dot(a, b, preferred_element_type=F32)


def _dot_nt(a, b):
    return lax.dot_general(a, b, (((1,), (1,)), ((), ())), preferred_element_type=F32)


def _tile(n, pref):
    return pref if n % pref == 0 else n


def _params(semantics):
    return pltpu.CompilerParams(dimension_semantics=semantics, vmem_limit_bytes=VMEM_LIMIT)


def _rms(x, g):
    return (x * lax.rsqrt(jnp.mean(x * x, axis=-1, keepdims=True) + EPS)) * g


def _norm_mod(x, g, shift, scale):
    return _rms(x, g) * (1.0 + scale) + shift


def _rope(a, cos, sin):
    return a * cos + pltpu.roll(a, HALF_HEAD, 1) * sin


def _dot_chunks(h_ref, w_ref, epilogue, chunk=MXU_COLS):
    n = w_ref.shape[1]
    chunk = chunk if n % chunk == 0 else n
    for c0 in range(0, n, chunk):
        cols = slice(c0, c0 + chunk)
        epilogue(cols, _dot(h_ref[...], w_ref[:, cols]))


def _cond_kernel(c_ref, w_ref, b_ref, o_ref):
    cond = jax.nn.silu(c_ref[...]).astype(BF16)
    o_ref[...] = _dot(cond, w_ref[...].astype(BF16)) + b_ref[...]


def _cond_matmul(c, w, b):
    n, d, e = w.shape
    bsz = c.shape[0]
    te = _tile(e, 1024)
    return pl.pallas_call(
        _cond_kernel,
        grid=(n, e // te),
        in_specs=[
            pl.BlockSpec((bsz, d), lambda s, j: (0, 0)),
            pl.BlockSpec((None, d, te), lambda s, j: (s, 0, j)),
            pl.BlockSpec((None, 1, te), lambda s, j: (s, 0, j)),
        ],
        out_specs=pl.BlockSpec((None, bsz, te), lambda s, j: (s, 0, j)),
        out_shape=jax.ShapeDtypeStruct((n, bsz, e), F32),
        compiler_params=_params(("parallel", "parallel")),
        name="cond_matmul",
    )(c, w, b.reshape(n, 1, e))


def _gmlp_kernel(x_ref, g_ref, sh_ref, sc_ref, gate_ref, win_ref, bin_ref, lg_ref, lb_ref, ws_ref,
                 bs_ref, wout_ref, bout_ref, o_ref, h_scr, v_scr, mix_scr, mu_scr, rs_scr,
                 *, n_half, gdim, chunk):
    j = pl.program_id(2)
    tm = v_scr.shape[0]
    tn = win_ref.shape[1]
    tn_out = wout_ref.shape[1]

    @pl.when(j == 0)
    def _():
        h_scr[...] = _norm_mod(x_ref[...], g_ref[...], sh_ref[...], sc_ref[...]).astype(BF16)

    @pl.when(j < n_half)
    def _():
        base = pl.multiple_of(j * tn, tn)

        def store(cols, acc):
            v_scr[:, pl.ds(base + cols.start, cols.stop - cols.start)] = jax.nn.gelu(acc + bin_ref[:, cols])
        _dot_chunks(h_scr, win_ref, store)

    @pl.when(j == n_half - 1)
    def _():
        v = v_scr[...]
        mu = jnp.mean(v, axis=-1, keepdims=True)
        dv = v - mu
        mu_scr[...] = mu
        rs_scr[...] = lax.rsqrt(jnp.mean(dv * dv, axis=-1, keepdims=True) + EPS)

    @pl.when((j >= n_half) & (j < 2 * n_half))
    def _():
        jb = j - n_half
        base = pl.multiple_of(jb * tn, tn)
        causal = (lax.broadcasted_iota(jnp.int32, (chunk, chunk), 1)
                  <= lax.broadcasted_iota(jnp.int32, (chunk, chunk), 0))

        def store(cols, acc):
            u = jax.nn.gelu(acc + bin_ref[:, cols])
            vcols = pl.ds(base + cols.start, gdim)
            vn = (((v_scr[:, vcols] - mu_scr[...]) * rs_scr[...]) * lg_ref[:, cols]
                  + lb_ref[:, cols]).astype(BF16)
            gi = jb * (tn // gdim) + cols.start // gdim
            w = jnp.where(causal, ws_ref[gi], 0.0).astype(BF16)
            for r0 in range(0, tm, chunk):
                rows = slice(r0, r0 + chunk)
                mixed = _dot(w, vn[rows]) + bs_ref[gi]
                mix_scr[rows, vcols] = (u[rows] * mixed).astype(BF16)
        _dot_chunks(h_scr, win_ref, store, chunk=gdim)

    @pl.when(j >= 2 * n_half)
    def _():
        base = pl.multiple_of((j - 2 * n_half) * tn_out, tn_out)

        def store(cols, acc):
            xcols = pl.ds(base + cols.start, cols.stop - cols.start)
            o_ref[:, cols] = x_ref[:, xcols] + gate_ref[:, cols] * (acc + bout_ref[:, cols])
        _dot_chunks(mix_scr, wout_ref, store)


def _gmlp(x, g, shift, scale, gate, w_in, b_in, ln_g, ln_b, w_s, b_s, w_out, b_out):
    bsz, s, d = x.shape
    width = w_in.shape[1] // 2
    groups, chunk, _ = w_s.shape
    gdim = width // groups
    tm = _tile(s, 512)
    tn = _tile(width, 1024)
    tn_out = _tile(d, 512)
    n_half = width // tn
    n_out = d // tn_out
    assert tn % gdim == 0 and gdim % LANES == 0 and tm % chunk == 0

    def in_col(j):
        return jnp.where(j < n_half, j + n_half, jnp.minimum(j - n_half, n_half - 1))

    def u_col(j):
        return jnp.clip(j - n_half, 0, n_half - 1)

    def out_col(j):
        return jnp.clip(j - 2 * n_half, 0, n_out - 1)

    row = lambda bi, i, j: (bi, 0, 0)
    return pl.pallas_call(
        functools.partial(_gmlp_kernel, n_half=n_half, gdim=gdim, chunk=chunk),
        grid=(bsz, s // tm, 2 * n_half + n_out),
        in_specs=[
            pl.BlockSpec((None, tm, d), lambda bi, i, j: (bi, i, 0)),
            pl.BlockSpec((1, d), lambda bi, i, j: (0, 0)),
            pl.BlockSpec((None, 1, d), row),
            pl.BlockSpec((None, 1, d), row),
            pl.BlockSpec((None, 1, tn_out), lambda bi, i, j: (bi, 0, out_col(j))),
            pl.BlockSpec((d, tn), lambda bi, i, j: (0, in_col(j))),
            pl.BlockSpec((1, tn), lambda bi, i, j: (0, in_col(j))),
            pl.BlockSpec((1, tn), lambda bi, i, j: (0, u_col(j))),
            pl.BlockSpec((1, tn), lambda bi, i, j: (0, u_col(j))),
            pl.BlockSpec((groups, chunk, chunk), lambda bi, i, j: (0, 0, 0)),
            pl.BlockSpec((groups, chunk, 1), lambda bi, i, j: (0, 0, 0)),
            pl.BlockSpec((width, tn_out), lambda bi, i, j: (0, out_col(j))),
            pl.BlockSpec((1, tn_out), lambda bi, i, j: (0, out_col(j))),
        ],
        out_specs=pl.BlockSpec((None, tm, tn_out), lambda bi, i, j: (bi, i, out_col(j))),
        out_shape=jax.ShapeDtypeStruct((bsz, s, d), F32),
        scratch_shapes=[pltpu.VMEM((tm, d), BF16),
                        pltpu.VMEM((tm, width), F32),
                        pltpu.VMEM((tm, width), BF16),
                        pltpu.VMEM((tm, 1), F32), pltpu.VMEM((tm, 1), F32)],
        compiler_params=_params(("parallel", "parallel", "arbitrary")),
        name="gmlp",
    )(x, g.reshape(1, d), shift, scale, gate, w_in, b_in.reshape(1, 2 * width),
      ln_g.reshape(1, width), ln_b.reshape(1, width), w_s, b_s.reshape(groups, chunk, 1),
      w_out, b_out.reshape(1, d))


def _mm_res_kernel(a_ref, w_ref, x_ref, gate_ref, o_ref):
    def store(cols, acc):
        o_ref[:, cols] = x_ref[:, cols] + gate_ref[:, cols] * acc
    _dot_chunks(a_ref, w_ref, store)


def _mm_residual(a, w, x, gate):
    bsz, s, k = a.shape
    d = w.shape[1]
    tm = _tile(s, 1024)
    tn = _tile(d, 512)
    return pl.pallas_call(
        _mm_res_kernel,
        grid=(bsz, s // tm, d // tn),
        in_specs=[
            pl.BlockSpec((None, tm, k), lambda bi, i, j: (bi, i, 0)),
            pl.BlockSpec((k, tn), lambda bi, i, j: (0, j)),
            pl.BlockSpec((None, tm, tn), lambda bi, i, j: (bi, i, j)),
            pl.BlockSpec((None, 1, tn), lambda bi, i, j: (bi, 0, j)),
        ],
        out_specs=pl.BlockSpec((None, tm, tn), lambda bi, i, j: (bi, i, j)),
        out_shape=jax.ShapeDtypeStruct((bsz, s, d), F32),
        compiler_params=_params(("parallel", "parallel", "parallel")),
        name="matmul_residual",
    )(a, w, x, gate)


def _mlp_kernel(x_ref, g_ref, sh_ref, sc_ref, gate_ref, w1_ref, w2_ref, fg_ref, o_ref,
                h_scr, acc_scr, *, final_norm):
    f = pl.program_id(2)

    @pl.when(f == 0)
    def _():
        h_scr[...] = _norm_mod(x_ref[...], g_ref[...], sh_ref[...], sc_ref[...]).astype(BF16)
        acc_scr[...] = jnp.zeros_like(acc_scr)

    a = jnp.square(jnp.maximum(_dot(h_scr[...], w1_ref[...]), 0.0)).astype(BF16)
    acc_scr[...] += _dot(a, w2_ref[...])

    @pl.when(f == pl.num_programs(2) - 1)
    def _():
        xn = x_ref[...] + gate_ref[...] * acc_scr[...]
        if final_norm:
            xn = _rms(xn, fg_ref[...])
        o_ref[...] = xn


def _mlp(x, g, shift, scale, gate, w1, w2, final_g, final_norm):
    bsz, s, d = x.shape
    dff = w1.shape[1]
    tm = _tile(s, 512)
    tf = _tile(dff, 1024)
    row = lambda bi, i, f: (bi, 0, 0)
    vec = pl.BlockSpec((1, d), lambda bi, i, f: (0, 0))
    xblk = pl.BlockSpec((None, tm, d), lambda bi, i, f: (bi, i, 0))
    return pl.pallas_call(
        functools.partial(_mlp_kernel, final_norm=final_norm),
        grid=(bsz, s // tm, dff // tf),
        in_specs=[
            xblk, vec,
            pl.BlockSpec((None, 1, d), row), pl.BlockSpec((None, 1, d), row),
            pl.BlockSpec((None, 1, d), row),
            pl.BlockSpec((d, tf), lambda bi, i, f: (0, f)),
            pl.BlockSpec((tf, d), lambda bi, i, f: (f, 0)),
            vec,
        ],
        out_specs=xblk,
        out_shape=jax.ShapeDtypeStruct((bsz, s, d), F32),
        scratch_shapes=[pltpu.VMEM((tm, d), BF16), pltpu.VMEM((tm, d), F32)],
        compiler_params=_params(("parallel", "parallel", "arbitrary")),
        name="mlp",
    )(x, g.reshape(1, d), shift, scale, gate, w1, w2, final_g.reshape(1, d))


def _kv_kernel(x_ref, g_ref, sh_ref, sc_ref, w_ref, cos_ref, sin_ref, cmp_ref, rest_ref, h_scr,
               *, groups):
    j = pl.program_id(2)

    @pl.when(j == 0)
    def _():
        h_scr[...] = _norm_mod(x_ref[...], g_ref[...], sh_ref[...], sc_ref[...]).astype(BF16)

    def emit(dst, rotate):
        def store(cols, acc):
            for c0 in range(0, acc.shape[1], HEAD_DIM):
                a = acc[:, c0:c0 + HEAD_DIM]
                if rotate:
                    a = _rope(a, cos_ref[...], sin_ref[...])
                dst[(cols.start + c0) // HEAD_DIM] = a.astype(dst.dtype)
        _dot_chunks(h_scr, w_ref, store)

    @pl.when(j == 0)
    def _():
        emit(cmp_ref, True)

    @pl.when(j == 1)
    def _():
        emit(cmp_ref, False)

    @pl.when((j >= 2) & (j % 2 == 0))
    def _():
        emit(rest_ref, True)

    @pl.when((j >= 2) & (j % 2 == 1))
    def _():
        emit(rest_ref, False)


def _kv_proj(x, g, shift, scale, w, cos, sin, groups):
    bsz, s, d = x.shape
    tm = _tile(s, 1024)
    tn = groups * HEAD_DIM
    row = lambda bi, i, j: (bi, 0, 0)
    tab = pl.BlockSpec((tm, HEAD_DIM), lambda bi, i, j: (i, 0))
    return pl.pallas_call(
        functools.partial(_kv_kernel, groups=groups),
        grid=(bsz, s // tm, 2 * N_BRANCHES),
        in_specs=[
            pl.BlockSpec((None, tm, d), lambda bi, i, j: (bi, i, 0)),
            pl.BlockSpec((1, d), lambda bi, i, j: (0, 0)),
            pl.BlockSpec((None, 1, d), row), pl.BlockSpec((None, 1, d), row),
            pl.BlockSpec((d, tn), lambda bi, i, j: (0, j)),
            tab, tab,
        ],
        out_specs=[
            pl.BlockSpec((None, None, groups, tm, HEAD_DIM),
                         lambda bi, i, j: (bi, jnp.minimum(j, 1), 0, i, 0)),
            pl.BlockSpec((None, None, groups, tm, HEAD_DIM),
                         lambda bi, i, j: (bi, jnp.maximum(j - 2, 0), 0, i, 0)),
        ],
        out_shape=[jax.ShapeDtypeStruct((bsz, 2, groups, s, HEAD_DIM), F32),
                   jax.ShapeDtypeStruct((bsz, 4, groups, s, HEAD_DIM), BF16)],
        scratch_shapes=[pltpu.VMEM((tm, d), BF16)],
        compiler_params=_params(("parallel", "parallel", "arbitrary")),
        name="kv_proj",
    )(x, g.reshape(1, d), shift, scale, w, cos, sin)


def _qg_kernel(x_ref, g_ref, sh_ref, sc_ref, w_ref, cos_ref, sin_ref, q_ref, gate_ref, h_scr,
               *, heads_per_tile, n_q):
    j = pl.program_id(2)

    @pl.when(j == 0)
    def _():
        h_scr[...] = _norm_mod(x_ref[...], g_ref[...], sh_ref[...], sc_ref[...]).astype(BF16)

    @pl.when(j < n_q)
    def _():
        def store(cols, acc):
            for c0 in range(0, acc.shape[1], HEAD_DIM):
                a = _rope(acc[:, c0:c0 + HEAD_DIM], cos_ref[...], sin_ref[...]) * (HEAD_DIM ** -0.5)
                q_ref[:, cols.start + c0:cols.start + c0 + HEAD_DIM] = a.astype(BF16)
        _dot_chunks(h_scr, w_ref, store)

    @pl.when(j >= n_q)
    def _():
        def store(cols, acc):
            gate_ref[:, cols] = jax.nn.sigmoid(acc)
        _dot_chunks(h_scr, w_ref, store)


def _qg_proj(x, g, shift, scale, w_q, w_gate, cos, sin, tn):
    bsz, s, d = x.shape
    nq_cols, ng_cols = w_q.shape[1], w_gate.shape[1]
    n_q, n_g = nq_cols // tn, ng_cols // tn
    tm = _tile(s, 1024)
    row = lambda bi, i, j: (bi, 0, 0)
    tab = pl.BlockSpec((tm, HEAD_DIM), lambda bi, i, j: (i, 0))
    w = jnp.concatenate([w_q, w_gate], axis=1)
    return pl.pallas_call(
        functools.partial(_qg_kernel, heads_per_tile=tn // HEAD_DIM, n_q=n_q),
        grid=(bsz, s // tm, n_q + n_g),
        in_specs=[
            pl.BlockSpec((None, tm, d), lambda bi, i, j: (bi, i, 0)),
            pl.BlockSpec((1, d), lambda bi, i, j: (0, 0)),
            pl.BlockSpec((None, 1, d), row), pl.BlockSpec((None, 1, d), row),
            pl.BlockSpec((d, tn), lambda bi, i, j: (0, j)),
            tab, tab,
        ],
        out_specs=[
            pl.BlockSpec((None, tm, tn), lambda bi, i, j: (bi, i, jnp.minimum(j, n_q - 1))),
            pl.BlockSpec((None, tm, tn), lambda bi, i, j: (bi, i, jnp.maximum(j - n_q, 0))),
        ],
        out_shape=[jax.ShapeDtypeStruct((bsz, s, nq_cols), BF16),
                   jax.ShapeDtypeStruct((bsz, s, ng_cols), F32)],
        scratch_shapes=[pltpu.VMEM((tm, d), BF16)],
        compiler_params=_params(("parallel", "parallel", "arbitrary")),
        name="qg_proj",
    )(x, g.reshape(1, d), shift, scale, w, cos, sin)


def _compress_kernel(h_ref, pos_ref, w1_ref, w2_ref, w2t_ref, o_ref, ot_ref):
    hv = h_ref[...]
    half = hv.shape[1]
    lo = (hv + pos_ref[0:1, :]).astype(BF16)
    hi = (hv + pos_ref[1:2, :]).astype(BF16)
    first = _dot(lo, w1_ref[0:half, :])
    second = _dot(hi, w1_ref[half:2 * half, :])
    n_rows = hv.shape[0]
    pre = first + pltpu.roll(second, n_rows - 1, 0)
    hidden = jax.nn.gelu(pre).astype(BF16)
    o_ref[...] = _dot(hidden, w2_ref[...]).astype(BF16)
    ot_ref[...] = _dot_nt(w2t_ref[...], hidden).astype(BF16)


def _compress(kv_cmp, pos, w1, w2):
    bsz, _, groups, s, _ = kv_cmp.shape
    n_rows = s // CMP_STRIDE
    flat = CMP_STRIDE * HEAD_DIM
    hidden = w1.shape[2]
    h = kv_cmp.reshape(bsz, 2, groups, n_rows, flat)
    return pl.pallas_call(
        _compress_kernel,
        grid=(bsz, 2, groups),
        in_specs=[
            pl.BlockSpec((None, None, None, n_rows, flat), lambda bi, t, gi: (bi, t, gi, 0, 0)),
            pl.BlockSpec((None, 2, flat), lambda bi, t, gi: (t, 0, 0)),
            pl.BlockSpec((None, 2 * flat, hidden), lambda bi, t, gi: (t, 0, 0)),
            pl.BlockSpec((None, hidden, HEAD_DIM), lambda bi, t, gi: (t, 0, 0)),
            pl.BlockSpec((None, HEAD_DIM, hidden), lambda bi, t, gi: (t, 0, 0)),
        ],
        out_specs=[pl.BlockSpec((None, None, None, n_rows, HEAD_DIM), lambda bi, t, gi: (bi, t, gi, 0, 0)),
                   pl.BlockSpec((None, None, None, HEAD_DIM, n_rows), lambda bi, t, gi: (bi, t, gi, 0, 0))],
        out_shape=[jax.ShapeDtypeStruct((bsz, 2, groups, n_rows, HEAD_DIM), BF16),
                   jax.ShapeDtypeStruct((bsz, 2, groups, HEAD_DIM, n_rows), BF16)],
        compiler_params=_params(("parallel", "parallel", "parallel")),
        name="compress",
    )(h, pos, w1, w2, jnp.swapaxes(w2, 1, 2))


def _masked_softmax(s, mask, axis):
    s = jnp.where(mask, s, NEG_INF)
    m = jnp.max(s, axis=axis, keepdims=True)
    e = jnp.where(mask, jnp.exp(s - m), 0.0)
    den = jnp.sum(e, axis=axis, keepdims=True)
    return e / jnp.maximum(den, 1e-30)


def _attn_kernel(q_ref, gate_ref, kc_ref, vct_ref, ks_ref, vst_ref, kw_ref, vwt_ref, ovt_ref,
                 o_ref, q4_scr, sel_scr, s_scr, m_scr, acc_scr, oc_scr, ow_scr,
                 *, hg, n_cmp, n_slc, n_sel):
    tq = ATTN_TILE
    tk = SLC_TILE
    i = pl.program_id(2)
    t0 = i * tq
    for h in range(hg):
        q4_scr[h * tq:(h + 1) * tq, :] = q_ref[:, h * HEAD_DIM:(h + 1) * HEAD_DIM]
    q4 = q4_scr[...]
    tpos_l = t0 + lax.broadcasted_iota(jnp.int32, (1, tq), 1)
    tpos4_l = jnp.concatenate([tpos_l] * hg, axis=1)

    def heads(a, h):
        return a[:, h * tq:(h + 1) * tq]

    n_idx = lax.broadcasted_iota(jnp.int32, (kc_ref.shape[0], 1), 0)
    valid = (n_idx * CMP_STRIDE + (CMP_BLOCK - 1) <= tpos4_l) & (n_idx < n_cmp)
    p_cmp = _masked_softmax(_dot_nt(kc_ref[...], q4), valid, 0).astype(BF16)
    oc_scr[...] = _dot(vct_ref[...], p_cmp)

    p_heads = jnp.concatenate([heads(p_cmp, h) for h in range(hg)], axis=0)
    imp = _dot(ovt_ref[...], p_heads)[:n_slc]
    blk = lax.broadcasted_iota(jnp.int32, (n_slc, 1), 0)
    cur = tpos_l // SEL_BLOCK
    forced = (blk == 0) | (blk == cur) | (blk == cur - 1)
    imp = jnp.where(blk * SEL_BLOCK <= tpos_l, imp + jnp.where(forced, FORCE_BONUS, 0.0), NEG_INF)
    rank = jnp.zeros((n_slc, tq), jnp.int32)
    for c in range(n_slc):
        row = imp[c:c + 1, :]
        beats = (row > imp) | ((row == imp) & (c < blk))
        rank = rank + jnp.where(beats, 1, 0)
    sel_scr[...] = jnp.where(rank < n_sel, 1.0, 0.0)

    def with_ones(vt):
        return jnp.concatenate([vt, jnp.ones(vt.shape, BF16)], axis=0)

    def normalised(acc):
        return acc[:HEAD_DIM] / jnp.maximum(acc[HEAD_DIM:], 1e-30)

    def sublane_tile_max(sh):
        return jnp.max(sh.reshape(sh.shape[0] // SUBLANES, SUBLANES, sh.shape[1]), axis=0)

    wk = WINDOW + tq
    kw0 = pl.multiple_of(jnp.maximum(t0 - WINDOW, 0), tq)
    kpos = kw0 + lax.broadcasted_iota(jnp.int32, (wk, 1), 0)
    bias = jnp.where((kpos <= tpos_l) & (kpos > tpos_l - WINDOW), 0.0, NEG_INF)
    s4 = _dot_nt(kw_ref[pl.ds(kw0, wk), :], q4)
    e = []
    for h in range(hg):
        sh = heads(s4, h) + bias
        m = jnp.max(sublane_tile_max(sh), axis=0, keepdims=True)
        e.append(jnp.exp(sh - m).astype(BF16))
    ow_scr[...] = normalised(_dot(with_ones(vwt_ref[:, pl.ds(kw0, wk)]), jnp.concatenate(e, axis=1)))

    n_tiles = i // (tk // tq) + 1
    blocks_per_tile = tk // SEL_BLOCK
    m_scr[...] = jnp.full_like(m_scr, NEG_INF)

    def scores(kt, carry):
        k0 = pl.multiple_of(kt * tk, tk)
        sel = sel_scr[pl.ds(pl.multiple_of(kt * blocks_per_tile, blocks_per_tile), blocks_per_tile), :]
        bias = []
        for jb in range(blocks_per_tile):
            kpos = k0 + jb * SEL_BLOCK + lax.broadcasted_iota(jnp.int32, (SEL_BLOCK, 1), 0)
            bias.append(jnp.where((sel[jb:jb + 1, :] > 0.5) & (kpos <= tpos_l), 0.0, NEG_INF))
        bias = jnp.concatenate(bias, axis=0)
        s4 = _dot_nt(ks_ref[pl.ds(k0, tk), :], q4)
        for h in range(hg):
            cols = slice(h * tq, (h + 1) * tq)
            sh = heads(s4, h) + bias
            s_scr[pl.ds(k0, tk), cols] = sh
            m_scr[:, cols] = jnp.maximum(m_scr[:, cols], sublane_tile_max(sh))
        return carry

    lax.fori_loop(0, n_tiles, scores, 0)
    m_scr[...] = jnp.broadcast_to(jnp.max(m_scr[...], axis=0, keepdims=True), m_scr.shape)
    acc_scr[...] = jnp.zeros_like(acc_scr)

    def weighted(kt, carry):
        k0 = pl.multiple_of(kt * tk, tk)
        s = s_scr[pl.ds(k0, tk), :].reshape(tk // SUBLANES, SUBLANES, hg * tq)
        e = jnp.exp(s - m_scr[...]).reshape(tk, hg * tq).astype(BF16)
        acc_scr[...] += _dot(with_ones(vst_ref[:, pl.ds(k0, tk)]), e)
        return carry

    lax.fori_loop(0, n_tiles, weighted, 0)

    o_slc = normalised(acc_scr[...])
    gates_t = gate_ref[...].T
    for h in range(hg):
        c0 = h * N_BRANCHES
        o = (gates_t[c0:c0 + 1] * heads(oc_scr, h) + gates_t[c0 + 1:c0 + 2] * heads(o_slc, h)
             + gates_t[c0 + 2:c0 + 3] * heads(ow_scr, h))
        o_ref[:, h * HEAD_DIM:(h + 1) * HEAD_DIM] = o.T.astype(BF16)


def _attention(q, gates, kv_cmp, kv_cmp_t, kv_rest, v_rest_t, hg):
    bsz, s, _ = q.shape
    groups = kv_rest.shape[2]
    n_rows = kv_cmp.shape[3]
    n_cmp = n_rows - 1
    n_slc = s // SEL_BLOCK
    n_sel = min(N_SELECT, n_slc)
    tq = ATTN_TILE
    assert n_rows == LANES and n_slc <= LANES and n_slc % (SLC_TILE // SEL_BLOCK) == 0
    assert s % SLC_TILE == 0 and s >= WINDOW + tq

    ci = np.arange(LANES)[None, :]
    sj = np.arange(LANES)[:, None]
    overlap_t = ((ci * CMP_STRIDE <= sj * SEL_BLOCK + SEL_BLOCK - 1)
                 & (ci * CMP_STRIDE + CMP_BLOCK - 1 >= sj * SEL_BLOCK) & (ci < n_cmp) & (sj < n_slc))
    overlap_t = jnp.asarray(np.tile(overlap_t, (1, hg)), BF16)

    def k_spec(which):
        return pl.BlockSpec((None, None, None, s, HEAD_DIM), lambda bi, gi, i: (bi, which, gi, 0, 0))

    def vt_spec(which):
        return pl.BlockSpec((None, None, None, HEAD_DIM, s), lambda bi, gi, i: (bi, which, gi, 0, 0))

    def cmp_spec(which):
        return pl.BlockSpec((None, None, None, n_rows, HEAD_DIM), lambda bi, gi, i: (bi, which, gi, 0, 0))

    qblk = pl.BlockSpec((None, tq, hg * HEAD_DIM), lambda bi, gi, i: (bi, i, gi))
    return pl.pallas_call(
        functools.partial(_attn_kernel, hg=hg, n_cmp=n_cmp, n_slc=n_slc, n_sel=n_sel),
        grid=(bsz, groups, s // tq),
        in_specs=[
            qblk,
            pl.BlockSpec((None, tq, LANES), lambda bi, gi, i: (bi, i, gi)),
            cmp_spec(0), cmp_spec(1),
            k_spec(0), vt_spec(0), k_spec(2), vt_spec(1),
            pl.BlockSpec((LANES, hg * LANES), lambda bi, gi, i: (0, 0)),
        ],
        out_specs=qblk,
        out_shape=jax.ShapeDtypeStruct(q.shape, BF16),
        scratch_shapes=[pltpu.VMEM((hg * tq, HEAD_DIM), BF16),
                        pltpu.VMEM((n_slc, tq), F32),
                        pltpu.VMEM((s, hg * tq), F32),
                        pltpu.VMEM((SUBLANES, hg * tq), F32),
                        pltpu.VMEM((2 * HEAD_DIM, hg * tq), F32),
                        pltpu.VMEM((HEAD_DIM, hg * tq), F32),
                        pltpu.VMEM((HEAD_DIM, hg * tq), F32)],
        compiler_params=_params(("parallel", "parallel", "arbitrary")),
        name="nsa_attention",
    )(q, gates, kv_cmp, kv_cmp_t, kv_rest, v_rest_t, kv_rest, v_rest_t, overlap_t)


def _rope_tables(s):
    freqs = ROPE_THETA ** (-jnp.arange(HALF_HEAD, dtype=F32) / HALF_HEAD)
    ang = jnp.arange(s).astype(F32)[:, None] * freqs[None, :]
    cos, sin = jnp.cos(ang), jnp.sin(ang)
    return jnp.concatenate([cos, cos], axis=1), jnp.concatenate([-sin, sin], axis=1)


def kernel(x, c, mod_w, mod_b, norm_g, mlp_w1, mlp_w2, a_w_in, a_b_in, a_ln_g, a_ln_b, a_w_s, a_b_s, a_w_out, a_b_out, kv_norm_g, kv_mod_w, kv_mod_b, w_kv, cmp_pos_k, cmp_w1_k, cmp_w2_k, cmp_pos_v, cmp_w1_v, cmp_w2_v, b_w_qg, b_w_o, final_g):
    bsz, s, d = x.shape
    depth = mod_w.shape[0]
    n_a = a_w_in.shape[0]
    groups = w_kv.shape[1] // (2 * N_BRANCHES * HEAD_DIM)
    heads = b_w_o.shape[1] // HEAD_DIM
    hg = heads // groups
    assert hg * N_BRANCHES <= LANES

    mod = _cond_matmul(c, mod_w.reshape(depth * 2, d, 3 * d), mod_b.reshape(depth * 2, 3 * d))
    mod = mod.reshape(depth, 2, bsz, 3, 1, d)
    kv_mod = _cond_matmul(c, kv_mod_w[None], kv_mod_b[None]).reshape(bsz, 2, 1, d)
    cos, sin = _rope_tables(s)

    for layer in range(depth):
        shift, scale, gate = (mod[layer, 0, :, t] for t in range(3))
        if layer < n_a:
            x = _gmlp(x, norm_g[layer, 0], shift, scale, gate,
                      a_w_in[layer].astype(BF16), a_b_in[layer], a_ln_g[layer], a_ln_b[layer],
                      a_w_s[layer], a_b_s[layer], a_w_out[layer].astype(BF16), a_b_out[layer])
        else:
            j = layer - n_a
            if layer == n_a:
                kv_cmp, kv_rest = _kv_proj(x, kv_norm_g, kv_mod[:, 0], kv_mod[:, 1],
                                           w_kv.astype(BF16), cos, sin, groups)
                pos = jnp.stack([cmp_pos_k, cmp_pos_v]).reshape(2, 2, CMP_STRIDE * HEAD_DIM)
                kv_cmp, kv_cmp_t = _compress(kv_cmp, pos,
                                             jnp.stack([cmp_w1_k, cmp_w1_v]).astype(BF16),
                                             jnp.stack([cmp_w2_k, cmp_w2_v]).astype(BF16))
                v_rest_t = jnp.swapaxes(kv_rest[:, 1::2], 3, 4)
            w_qg = b_w_qg[j]
            w_q = w_qg[:, :heads * HEAD_DIM]
            w_gate = w_qg[:, heads * HEAD_DIM:].reshape(d, groups, hg * N_BRANCHES)
            w_gate = jnp.pad(w_gate, ((0, 0), (0, 0), (0, LANES - hg * N_BRANCHES)))
            w_gate = w_gate.reshape(d, groups * LANES)
            tn = hg * HEAD_DIM
            pad = (-w_gate.shape[1]) % tn
            w_gate = jnp.pad(w_gate, ((0, 0), (0, pad)))
            q, gates = _qg_proj(x, norm_g[layer, 0], shift, scale,
                                w_q.astype(BF16), w_gate.astype(BF16), cos, sin, tn)
            o = _attention(q, gates, kv_cmp, kv_cmp_t, kv_rest, v_rest_t, hg)
            x = _mm_residual(o, b_w_o[j].astype(BF16), x, gate)

        shift, scale, gate = (mod[layer, 1, :, t] for t in range(3))
        x = _mlp(x, norm_g[layer, 1], shift, scale, gate,
                 mlp_w1[layer].astype(BF16), mlp_w2[layer].astype(BF16),
                 final_g, final_norm=(layer == depth - 1))
    return x
```

```python
import functools

import numpy as np
import jax
import jax.numpy as jnp
from jax import lax
from jax.experimental import pallas as pl
from jax.experimental.pallas import tpu as pltpu

BF16 = jnp.bfloat16
F32 = jnp.float32

LANES = 128
SUBLANES = 8
MXU_COLS = 256
HEAD_DIM = 128
HALF_HEAD = HEAD_DIM // 2
N_BRANCHES = 3
CMP_BLOCK = 32
CMP_STRIDE = 16
SEL_BLOCK = 64
N_SELECT = 8
WINDOW = 512
ROPE_THETA = 10000.0
EPS = 1e-6
NEG_INF = -1e30
FORCE_BONUS = 1e6
LOG2_E = 1.4426950408889634
ATTN_TILE = 256
SLC_TILE = 512
VMEM_LIMIT = 56 * 1024 * 1024


def _dot(a, b):
    return jnp.dot(a, b, preferred_element_type=F32)


def _dot_nt(a, b):
    return lax.dot_general(a, b, (((1,), (1,)), ((), ())), preferred_element_type=F32)


def _tile(n, pref):
    return pref if n % pref == 0 else n


def _params(semantics):
    return pltpu.CompilerParams(dimension_semantics=semantics, vmem_limit_bytes=VMEM_LIMIT)


def _rms(x, g):
    return (x * lax.rsqrt(jnp.mean(x * x, axis=-1, keepdims=True) + EPS)) * g


def _norm_mod(x, g, shift, scale):
    return _rms(x, g) * (1.0 + scale) + shift


def _rope(a, cos, sin):
    return a * cos + pltpu.roll(a, HALF_HEAD, 1) * sin


def _dot_chunks(h_ref, w_ref, epilogue, chunk=MXU_COLS):
    n = w_ref.shape[1]
    chunk = chunk if n % chunk == 0 else n
    for c0 in range(0, n, chunk):
        cols = slice(c0, c0 + chunk)
        epilogue(cols, _dot(h_ref[...], w_ref[:, cols]))


def _cond_kernel(c_ref, w_ref, b_ref, o_ref):
    cond = jax.nn.silu(c_ref[...]).astype(BF16)
    o_ref[...] = _dot(cond, w_ref[...].astype(BF16)) + b_ref[...]


def _cond_matmul(c, w, b):
    n, d, e = w.shape
    bsz = c.shape[0]
    te = _tile(e, 1024)
    return pl.pallas_call(
        _cond_kernel,
        grid=(n, e // te),
        in_specs=[
            pl.BlockSpec((bsz, d), lambda s, j: (0, 0)),
            pl.BlockSpec((None, d, te), lambda s, j: (s, 0, j)),
            pl.BlockSpec((None, 1, te), lambda s, j: (s, 0, j)),
        ],
        out_specs=pl.BlockSpec((None, bsz, te), lambda s, j: (s, 0, j)),
        out_shape=jax.ShapeDtypeStruct((n, bsz, e), F32),
        compiler_params=_params(("parallel", "parallel")),
        name="cond_matmul",
    )(c, w, b.reshape(n, 1, e))


def _gmlp_kernel(x_ref, g_ref, sh_ref, sc_ref, gate_ref, win_ref, bin_ref, lg_ref, lb_ref, ws_ref,
                 bs_ref, wout_ref, bout_ref, o_ref, h_scr, v_scr, mix_scr, mu_scr, rs_scr,
                 *, n_half, gdim, chunk):
    j = pl.program_id(2)
    tm = v_scr.shape[0]
    tn = win_ref.shape[1]
    tn_out = wout_ref.shape[1]

    @pl.when(j == 0)
    def _():
        h_scr[...] = _norm_mod(x_ref[...], g_ref[...], sh_ref[...], sc_ref[...]).astype(BF16)

    @pl.when(j < n_half)
    def _():
        base = pl.multiple_of(j * tn, tn)

        def store(cols, acc):
            v_scr[:, pl.ds(base + cols.start, cols.stop - cols.start)] = jax.nn.gelu(acc + bin_ref[:, cols])
        _dot_chunks(h_scr, win_ref, store)

    @pl.when(j == n_half - 1)
    def _():
        v = v_scr[...]
        mu = jnp.mean(v, axis=-1, keepdims=True)
        dv = v - mu
        mu_scr[...] = mu
        rs_scr[...] = lax.rsqrt(jnp.mean(dv * dv, axis=-1, keepdims=True) + EPS)

    @pl.when((j >= n_half) & (j < 2 * n_half))
    def _():
        jb = j - n_half
        base = pl.multiple_of(jb * tn, tn)
        causal = (lax.broadcasted_iota(jnp.int32, (chunk, chunk), 1)
                  <= lax.broadcasted_iota(jnp.int32, (chunk, chunk), 0))

        def store(cols, acc):
            u = jax.nn.gelu(acc + bin_ref[:, cols])
            vcols = pl.ds(base + cols.start, gdim)
            vn = (((v_scr[:, vcols] - mu_scr[...]) * rs_scr[...]) * lg_ref[:, cols]
                  + lb_ref[:, cols]).astype(BF16)
            gi = jb * (tn // gdim) + cols.start // gdim
            w = jnp.where(causal, ws_ref[gi], 0.0).astype(BF16)
            for r0 in range(0, tm, chunk):
                rows = slice(r0, r0 + chunk)
                mixed = _dot(w, vn[rows]) + bs_ref[gi]
                mix_scr[rows, vcols] = (u[rows] * mixed).astype(BF16)
        _dot_chunks(h_scr, win_ref, store, chunk=gdim)

    @pl.when(j >= 2 * n_half)
    def _():
        base = pl.multiple_of((j - 2 * n_half) * tn_out, tn_out)

        def store(cols, acc):
            xcols = pl.ds(base + cols.start, cols.stop - cols.start)
            o_ref[:, cols] = x_ref[:, xcols] + gate_ref[:, cols] * (acc + bout_ref[:, cols])
        _dot_chunks(mix_scr, wout_ref, store)


def _gmlp(x, g, shift, scale, gate, w_in, b_in, ln_g, ln_b, w_s, b_s, w_out, b_out):
    bsz, s, d = x.shape
    width = w_in.shape[1] // 2
    groups, chunk, _ = w_s.shape
    gdim = width // groups
    tm = _tile(s, 512)
    tn = _tile(width, 1024)
    tn_out = _tile(d, 512)
    n_half = width // tn
    n_out = d // tn_out
    assert tn % gdim == 0 and gdim % LANES == 0 and tm % chunk == 0

    def in_col(j):
        return jnp.where(j < n_half, j + n_half, jnp.minimum(j - n_half, n_half - 1))

    def u_col(j):
        return jnp.clip(j - n_half, 0, n_half - 1)

    def out_col(j):
        return jnp.clip(j - 2 * n_half, 0, n_out - 1)

    row = lambda bi, i, j: (bi, 0, 0)
    return pl.pallas_call(
        functools.partial(_gmlp_kernel, n_half=n_half, gdim=gdim, chunk=chunk),
        grid=(bsz, s // tm, 2 * n_half + n_out),
        in_specs=[
            pl.BlockSpec((None, tm, d), lambda bi, i, j: (bi, i, 0)),
            pl.BlockSpec((1, d), lambda bi, i, j: (0, 0)),
            pl.BlockSpec((None, 1, d), row),
            pl.BlockSpec((None, 1, d), row),
            pl.BlockSpec((None, 1, tn_out), lambda bi, i, j: (bi, 0, out_col(j))),
            pl.BlockSpec((d, tn), lambda bi, i, j: (0, in_col(j))),
            pl.BlockSpec((1, tn), lambda bi, i, j: (0, in_col(j))),
            pl.BlockSpec((1, tn), lambda bi, i, j: (0, u_col(j))),
            pl.BlockSpec((1, tn), lambda bi, i, j: (0, u_col(j))),
            pl.BlockSpec((groups, chunk, chunk), lambda bi, i, j: (0, 0, 0)),
            pl.BlockSpec((groups, chunk, 1), lambda bi, i, j: (0, 0, 0)),
            pl.BlockSpec((width, tn_out), lambda bi, i, j: (0, out_col(j))),
            pl.BlockSpec((1, tn_out), lambda bi, i, j: (0, out_col(j))),
        ],
        out_specs=pl.BlockSpec((None, tm, tn_out), lambda bi, i, j: (bi, i, out_col(j))),
        out_shape=jax.ShapeDtypeStruct((bsz, s, d), F32),
        scratch_shapes=[pltpu.VMEM((tm, d), BF16),
                        pltpu.VMEM((tm, width), F32),
                        pltpu.VMEM((tm, width), BF16),
                        pltpu.VMEM((tm, 1), F32), pltpu.VMEM((tm, 1), F32)],
        compiler_params=_params(("parallel", "parallel", "arbitrary")),
        name="gmlp",
    )(x, g.reshape(1, d), shift, scale, gate, w_in, b_in.reshape(1, 2 * width),
      ln_g.reshape(1, width), ln_b.reshape(1, width), w_s, b_s.reshape(groups, chunk, 1),
      w_out, b_out.reshape(1, d))


def _mm_res_kernel(a_ref, w_ref, x_ref, gate_ref, o_ref):
    def store(cols, acc):
        o_ref[:, cols] = x_ref[:, cols] + gate_ref[:, cols] * acc
    _dot_chunks(a_ref, w_ref, store)


def _mm_residual(a, w, x, gate):
    bsz, s, k = a.shape
    d = w.shape[1]
    tm = _tile(s, 1024)
    tn = _tile(d, 512)
    return pl.pallas_call(
        _mm_res_kernel,
        grid=(bsz, s // tm, d // tn),
        in_specs=[
            pl.BlockSpec((None, tm, k), lambda bi, i, j: (bi, i, 0)),
            pl.BlockSpec((k, tn), lambda bi, i, j: (0, j)),
            pl.BlockSpec((None, tm, tn), lambda bi, i, j: (bi, i, j)),
            pl.BlockSpec((None, 1, tn), lambda bi, i, j: (bi, 0, j)),
        ],
        out_specs=pl.BlockSpec((None, tm, tn), lambda bi, i, j: (bi, i, j)),
        out_shape=jax.ShapeDtypeStruct((bsz, s, d), F32),
        compiler_params=_params(("parallel", "parallel", "parallel")),
        name="matmul_residual",
    )(a, w, x, gate)


def _mlp_kernel(x_ref, g_ref, sh_ref, sc_ref, gate_ref, w1_ref, w2_ref, fg_ref, o_ref,
                h_scr, acc_scr, *, final_norm):
    f = pl.program_id(2)

    @pl.when(f == 0)
    def _():
        h_scr[...] = _norm_mod(x_ref[...], g_ref[...], sh_ref[...], sc_ref[...]).astype(BF16)
        acc_scr[...] = jnp.zeros_like(acc_scr)

    a = jnp.square(jnp.maximum(_dot(h_scr[...], w1_ref[...]), 0.0)).astype(BF16)
    acc_scr[...] += _dot(a, w2_ref[...])

    @pl.when(f == pl.num_programs(2) - 1)
    def _():
        xn = x_ref[...] + gate_ref[...] * acc_scr[...]
        if final_norm:
            xn = _rms(xn, fg_ref[...])
        o_ref[...] = xn


def _mlp(x, g, shift, scale, gate, w1, w2, layer, final_g, final_norm):
    bsz, s, d = x.shape
    dff = w1.shape[2]
    tm = _tile(s, 512)
    tf = _tile(dff, 1024)
    row = lambda bi, i, f: (bi, 0, 0)
    vec = pl.BlockSpec((1, d), lambda bi, i, f: (0, 0))
    xblk = pl.BlockSpec((None, tm, d), lambda bi, i, f: (bi, i, 0))
    return pl.pallas_call(
        functools.partial(_mlp_kernel, final_norm=final_norm),
        grid=(bsz, s // tm, dff // tf),
        in_specs=[
            xblk, vec,
            pl.BlockSpec((None, 1, d), row), pl.BlockSpec((None, 1, d), row),
            pl.BlockSpec((None, 1, d), row),
            pl.BlockSpec((None, d, tf), lambda bi, i, f: (layer, 0, f)),
            pl.BlockSpec((None, tf, d), lambda bi, i, f: (layer, f, 0)),
            vec,
        ],
        out_specs=xblk,
        out_shape=jax.ShapeDtypeStruct((bsz, s, d), F32),
        scratch_shapes=[pltpu.VMEM((tm, d), BF16), pltpu.VMEM((tm, d), F32)],
        compiler_params=_params(("parallel", "parallel", "arbitrary")),
        name="mlp",
    )(x, g.reshape(1, d), shift, scale, gate, w1, w2, final_g.reshape(1, d))


def _kv_kernel(x_ref, g_ref, sh_ref, sc_ref, w_ref, cos_ref, sin_ref, cmp_ref, rest_ref, h_scr,
               *, groups):
    j = pl.program_id(2)

    @pl.when(j == 0)
    def _():
        h_scr[...] = _norm_mod(x_ref[...], g_ref[...], sh_ref[...], sc_ref[...]).astype(BF16)

    def emit(dst, rotate):
        def store(cols, acc):
            for c0 in range(0, acc.shape[1], HEAD_DIM):
                a = acc[:, c0:c0 + HEAD_DIM]
                if rotate:
                    a = _rope(a, cos_ref[...], sin_ref[...])
                dst[(cols.start + c0) // HEAD_DIM] = a.astype(dst.dtype)
        _dot_chunks(h_scr, w_ref, store)

    @pl.when(j == 0)
    def _():
        emit(cmp_ref, True)

    @pl.when(j == 1)
    def _():
        emit(cmp_ref, False)

    @pl.when((j >= 2) & (j % 2 == 0))
    def _():
        emit(rest_ref, True)

    @pl.when((j >= 2) & (j % 2 == 1))
    def _():
        emit(rest_ref, False)


def _kv_proj(x, g, shift, scale, w, cos, sin, groups):
    bsz, s, d = x.shape
    tm = _tile(s, 1024)
    tn = groups * HEAD_DIM
    row = lambda bi, i, j: (bi, 0, 0)
    tab = pl.BlockSpec((tm, HEAD_DIM), lambda bi, i, j: (i, 0))
    return pl.pallas_call(
        functools.partial(_kv_kernel, groups=groups),
        grid=(bsz, s // tm, 2 * N_BRANCHES),
        in_specs=[
            pl.BlockSpec((None, tm, d), lambda bi, i, j: (bi, i, 0)),
            pl.BlockSpec((1, d), lambda bi, i, j: (0, 0)),
            pl.BlockSpec((None, 1, d), row), pl.BlockSpec((None, 1, d), row),
            pl.BlockSpec((d, tn), lambda bi, i, j: (0, j)),
            tab, tab,
        ],
        out_specs=[
            pl.BlockSpec((None, None, groups, tm, HEAD_DIM),
                         lambda bi, i, j: (bi, jnp.minimum(j, 1), 0, i, 0)),
            pl.BlockSpec((None, None, groups, tm, HEAD_DIM),
                         lambda bi, i, j: (bi, jnp.maximum(j - 2, 0), 0, i, 0)),
        ],
        out_shape=[jax.ShapeDtypeStruct((bsz, 2, groups, s, HEAD_DIM), F32),
                   jax.ShapeDtypeStruct((bsz, 4, groups, s, HEAD_DIM), BF16)],
        scratch_shapes=[pltpu.VMEM((tm, d), BF16)],
        compiler_params=_params(("parallel", "parallel", "arbitrary")),
        name="kv_proj",
    )(x, g.reshape(1, d), shift, scale, w, cos, sin)


def _qg_kernel(x_ref, g_ref, sh_ref, sc_ref, w_ref, cos_ref, sin_ref, q_ref, gate_ref, h_scr,
               *, heads_per_tile, n_q):
    j = pl.program_id(2)

    @pl.when(j == 0)
    def _():
        h_scr[...] = _norm_mod(x_ref[...], g_ref[...], sh_ref[...], sc_ref[...]).astype(BF16)

    @pl.when(j < n_q)
    def _():
        def store(cols, acc):
            for c0 in range(0, acc.shape[1], HEAD_DIM):
                a = _rope(acc[:, c0:c0 + HEAD_DIM], cos_ref[...], sin_ref[...]) * (HEAD_DIM ** -0.5)
                q_ref[:, cols.start + c0:cols.start + c0 + HEAD_DIM] = a.astype(BF16)
        _dot_chunks(h_scr, w_ref, store)

    @pl.when(j >= n_q)
    def _():
        def store(cols, acc):
            gate_ref[:, cols] = jax.nn.sigmoid(acc)
        _dot_chunks(h_scr, w_ref, store)


def _qg_proj(x, g, shift, scale, w_q, w_gate, cos, sin, tn):
    bsz, s, d = x.shape
    nq_cols, ng_cols = w_q.shape[1], w_gate.shape[1]
    n_q, n_g = nq_cols // tn, ng_cols // tn
    tm = _tile(s, 1024)
    row = lambda bi, i, j: (bi, 0, 0)
    tab = pl.BlockSpec((tm, HEAD_DIM), lambda bi, i, j: (i, 0))
    w = jnp.concatenate([w_q, w_gate], axis=1)
    return pl.pallas_call(
        functools.partial(_qg_kernel, heads_per_tile=tn // HEAD_DIM, n_q=n_q),
        grid=(bsz, s // tm, n_q + n_g),
        in_specs=[
            pl.BlockSpec((None, tm, d), lambda bi, i, j: (bi, i, 0)),
            pl.BlockSpec((1, d), lambda bi, i, j: (0, 0)),
            pl.BlockSpec((None, 1, d), row), pl.BlockSpec((None, 1, d), row),
            pl.BlockSpec((d, tn), lambda bi, i, j: (0, j)),
            tab, tab,
        ],
        out_specs=[
            pl.BlockSpec((None, tm, tn), lambda bi, i, j: (bi, i, jnp.minimum(j, n_q - 1))),
            pl.BlockSpec((None, tm, tn), lambda bi, i, j: (bi, i, jnp.maximum(j - n_q, 0))),
        ],
        out_shape=[jax.ShapeDtypeStruct((bsz, s, nq_cols), BF16),
                   jax.ShapeDtypeStruct((bsz, s, ng_cols), F32)],
        scratch_shapes=[pltpu.VMEM((tm, d), BF16)],
        compiler_params=_params(("parallel", "parallel", "arbitrary")),
        name="qg_proj",
    )(x, g.reshape(1, d), shift, scale, w, cos, sin)


def _compress_kernel(kv_ref, pos_ref, w1_ref, w2_ref, w2t_ref, o_ref, ot_ref):
    n_rows = kv_ref.shape[0] // CMP_STRIDE

    def half_block(p0):
        total = None
        for p in range(0, CMP_STRIDE, 2):
            a = jnp.concatenate(
                [kv_ref[pl.ds(p + q, n_rows, stride=CMP_STRIDE), :] + pos_ref[p0 + p + q:p0 + p + q + 1, :]
                 for q in range(2)], axis=1).astype(BF16)
            part = _dot(a, w1_ref[(p0 + p) * HEAD_DIM:(p0 + p + 2) * HEAD_DIM, :])
            total = part if total is None else total + part
        return total

    first = half_block(0)
    second = half_block(CMP_STRIDE)
    pre = first + pltpu.roll(second, n_rows - 1, 0)
    hidden = jax.nn.gelu(pre).astype(BF16)
    o_ref[...] = _dot(hidden, w2_ref[...]).astype(BF16)
    ot_ref[...] = _dot_nt(w2t_ref[...], hidden).astype(BF16)


def _compress(kv_cmp, pos, w1, w2):
    bsz, _, groups, s, _ = kv_cmp.shape
    n_rows = s // CMP_STRIDE
    hidden = w1.shape[2]
    return pl.pallas_call(
        _compress_kernel,
        grid=(bsz, 2, groups),
        in_specs=[
            pl.BlockSpec((None, None, None, s, HEAD_DIM), lambda bi, t, gi: (bi, t, gi, 0, 0)),
            pl.BlockSpec((None, CMP_BLOCK, HEAD_DIM), lambda bi, t, gi: (t, 0, 0)),
            pl.BlockSpec((None, CMP_BLOCK * HEAD_DIM, hidden), lambda bi, t, gi: (t, 0, 0)),
            pl.BlockSpec((None, hidden, HEAD_DIM), lambda bi, t, gi: (t, 0, 0)),
            pl.BlockSpec((None, HEAD_DIM, hidden), lambda bi, t, gi: (t, 0, 0)),
        ],
        out_specs=[pl.BlockSpec((None, None, None, n_rows, HEAD_DIM), lambda bi, t, gi: (bi, t, gi, 0, 0)),
                   pl.BlockSpec((None, None, None, HEAD_DIM, n_rows), lambda bi, t, gi: (bi, t, gi, 0, 0))],
        out_shape=[jax.ShapeDtypeStruct((bsz, 2, groups, n_rows, HEAD_DIM), BF16),
                   jax.ShapeDtypeStruct((bsz, 2, groups, HEAD_DIM, n_rows), BF16)],
        compiler_params=_params(("parallel", "parallel", "parallel")),
        name="compress",
    )(kv_cmp, pos, w1, w2, jnp.swapaxes(w2, 1, 2))


def _masked_softmax(s, mask, axis):
    s = jnp.where(mask, s, NEG_INF)
    m = jnp.max(s, axis=axis, keepdims=True)
    e = jnp.where(mask, jnp.exp(s - m), 0.0)
    den = jnp.sum(e, axis=axis, keepdims=True)
    return e / jnp.maximum(den, 1e-30)


def _attn_kernel(q_ref, gate_ref, kc_ref, vct_ref, ks_ref, vst_ref, kw_ref, vwt_ref, ovt_ref,
                 o_ref, q4_scr, sel_scr, s_scr, m_scr, acc_scr, oc_scr, ow_scr,
                 *, hg, n_cmp, n_slc, n_sel):
    tq = ATTN_TILE
    tk = SLC_TILE
    i = pl.program_id(2)
    t0 = i * tq
    for h in range(hg):
        q4_scr[h * tq:(h + 1) * tq, :] = q_ref[:, h * HEAD_DIM:(h + 1) * HEAD_DIM]
    q4 = q4_scr[...]
    tpos_l = t0 + lax.broadcasted_iota(jnp.int32, (1, tq), 1)
    tpos4_l = jnp.concatenate([tpos_l] * hg, axis=1)

    def heads(a, h):
        return a[:, h * tq:(h + 1) * tq]

    n_idx = lax.broadcasted_iota(jnp.int32, (kc_ref.shape[0], 1), 0)
    valid = (n_idx * CMP_STRIDE + (CMP_BLOCK - 1) <= tpos4_l) & (n_idx < n_cmp)
    p_cmp = _masked_softmax(_dot_nt(kc_ref[...], q4), valid, 0).astype(BF16)
    oc_scr[...] = _dot(vct_ref[...], p_cmp)

    p_heads = jnp.concatenate([heads(p_cmp, h) for h in range(hg)], axis=0)
    imp = _dot(ovt_ref[...], p_heads)[:n_slc]
    blk = lax.broadcasted_iota(jnp.int32, (n_slc, 1), 0)
    cur = tpos_l // SEL_BLOCK
    forced = (blk == 0) | (blk == cur) | (blk == cur - 1)
    imp = jnp.where(blk * SEL_BLOCK <= tpos_l, imp + jnp.where(forced, FORCE_BONUS, 0.0), NEG_INF)
    rank = jnp.zeros((n_slc, tq), jnp.int32)
    for c in range(n_slc):
        row = imp[c:c + 1, :]
        beats = (row > imp) | ((row == imp) & (c < blk))
        rank = rank + jnp.where(beats, 1, 0)
    sel_scr[...] = jnp.where(rank < n_sel, 1.0, 0.0)

    def with_ones(vt):
        return jnp.concatenate([vt, jnp.ones(vt.shape, BF16)], axis=0)

    def normalised(acc):
        return acc[:HEAD_DIM] / jnp.maximum(acc[HEAD_DIM:], 1e-30)

    def sublane_tile_max(sh):
        return jnp.max(sh.reshape(sh.shape[0] // SUBLANES, SUBLANES, sh.shape[1]), axis=0)

    wk = WINDOW + tq
    kw0 = pl.multiple_of(jnp.maximum(t0 - WINDOW, 0), tq)
    kpos = kw0 + lax.broadcasted_iota(jnp.int32, (wk, 1), 0)
    bias = jnp.where((kpos <= tpos_l) & (kpos > tpos_l - WINDOW), 0.0, NEG_INF)
    s4 = _dot_nt(kw_ref[pl.ds(kw0, wk), :], q4)
    e = []
    for h in range(hg):
        sh = heads(s4, h) + bias
        m = jnp.max(sublane_tile_max(sh), axis=0, keepdims=True)
        e.append(jnp.exp(sh - m).astype(BF16))
    ow_scr[...] = normalised(_dot(with_ones(vwt_ref[:, pl.ds(kw0, wk)]), jnp.concatenate(e, axis=1)))

    n_tiles = i // (tk // tq) + 1
    blocks_per_tile = tk // SEL_BLOCK
    m_scr[...] = jnp.full_like(m_scr, NEG_INF)

    def scores(kt, carry):
        k0 = pl.multiple_of(kt * tk, tk)
        sel = sel_scr[pl.ds(pl.multiple_of(kt * blocks_per_tile, blocks_per_tile), blocks_per_tile), :]
        bias = []
        for jb in range(blocks_per_tile):
            kpos = k0 + jb * SEL_BLOCK + lax.broadcasted_iota(jnp.int32, (SEL_BLOCK, 1), 0)
            bias.append(jnp.where((sel[jb:jb + 1, :] > 0.5) & (kpos <= tpos_l), 0.0, NEG_INF))
        bias = jnp.concatenate(bias, axis=0)
        s4 = _dot_nt(ks_ref[pl.ds(k0, tk), :], q4)
        for h in range(hg):
            cols = slice(h * tq, (h + 1) * tq)
            sh = (heads(s4, h) + bias) * LOG2_E
            s_scr[pl.ds(k0, tk), cols] = sh
            m_scr[:, cols] = jnp.maximum(m_scr[:, cols], sublane_tile_max(sh))
        return carry

    lax.fori_loop(0, n_tiles, scores, 0)
    m_scr[...] = jnp.broadcast_to(jnp.max(m_scr[...], axis=0, keepdims=True), m_scr.shape)
    acc_scr[...] = jnp.zeros_like(acc_scr)

    def weighted(kt, carry):
        k0 = pl.multiple_of(kt * tk, tk)
        s = s_scr[pl.ds(k0, tk), :].reshape(tk // SUBLANES, SUBLANES, hg * tq)
        e = jnp.exp2(s - m_scr[...]).reshape(tk, hg * tq).astype(BF16)
        acc_scr[...] += _dot(with_ones(vst_ref[:, pl.ds(k0, tk)]), e)
        return carry

    lax.fori_loop(0, n_tiles, weighted, 0)

    o_slc = normalised(acc_scr[...])
    gates_t = gate_ref[...].T
    for h in range(hg):
        c0 = h * N_BRANCHES
        o = (gates_t[c0:c0 + 1] * heads(oc_scr, h) + gates_t[c0 + 1:c0 + 2] * heads(o_slc, h)
             + gates_t[c0 + 2:c0 + 3] * heads(ow_scr, h))
        o_ref[:, h * HEAD_DIM:(h + 1) * HEAD_DIM] = o.T.astype(BF16)


def _attention(q, gates, kv_cmp, kv_cmp_t, kv_rest, v_rest_t, hg):
    bsz, s, _ = q.shape
    groups = kv_rest.shape[2]
    n_rows = kv_cmp.shape[3]
    n_cmp = n_rows - 1
    n_slc = s // SEL_BLOCK
    n_sel = min(N_SELECT, n_slc)
    tq = ATTN_TILE
    assert n_rows == LANES and n_slc <= LANES and n_slc % (SLC_TILE // SEL_BLOCK) == 0
    assert s % SLC_TILE == 0 and s >= WINDOW + tq

    ci = np.arange(LANES)[None, :]
    sj = np.arange(LANES)[:, None]
    overlap_t = ((ci * CMP_STRIDE <= sj * SEL_BLOCK + SEL_BLOCK - 1)
                 & (ci * CMP_STRIDE + CMP_BLOCK - 1 >= sj * SEL_BLOCK) & (ci < n_cmp) & (sj < n_slc))
    overlap_t = jnp.asarray(np.tile(overlap_t, (1, hg)), BF16)

    def k_spec(which):
        return pl.BlockSpec((None, None, None, s, HEAD_DIM), lambda bi, gi, i: (bi, which, gi, 0, 0))

    def vt_spec(which):
        return pl.BlockSpec((None, None, None, HEAD_DIM, s), lambda bi, gi, i: (bi, which, gi, 0, 0))

    def cmp_spec(which):
        return pl.BlockSpec((None, None, None, n_rows, HEAD_DIM), lambda bi, gi, i: (bi, which, gi, 0, 0))

    qblk = pl.BlockSpec((None, tq, hg * HEAD_DIM), lambda bi, gi, i: (bi, i, gi))
    return pl.pallas_call(
        functools.partial(_attn_kernel, hg=hg, n_cmp=n_cmp, n_slc=n_slc, n_sel=n_sel),
        grid=(bsz, groups, s // tq),
        in_specs=[
            qblk,
            pl.BlockSpec((None, tq, LANES), lambda bi, gi, i: (bi, i, gi)),
            cmp_spec(0), cmp_spec(1),
            k_spec(0), vt_spec(0), k_spec(2), vt_spec(1),
            pl.BlockSpec((LANES, hg * LANES), lambda bi, gi, i: (0, 0)),
        ],
        out_specs=qblk,
        out_shape=jax.ShapeDtypeStruct(q.shape, BF16),
        scratch_shapes=[pltpu.VMEM((hg * tq, HEAD_DIM), BF16),
                        pltpu.VMEM((n_slc, tq), F32),
                        pltpu.VMEM((s, hg * tq), F32),
                        pltpu.VMEM((SUBLANES, hg * tq), F32),
                        pltpu.VMEM((2 * HEAD_DIM, hg * tq), F32),
                        pltpu.VMEM((HEAD_DIM, hg * tq), F32),
                        pltpu.VMEM((HEAD_DIM, hg * tq), F32)],
        compiler_params=_params(("parallel", "parallel", "arbitrary")),
        name="nsa_attention",
    )(q, gates, kv_cmp, kv_cmp_t, kv_rest, v_rest_t, kv_rest, v_rest_t, overlap_t)


def _rope_tables(s):
    freqs = ROPE_THETA ** (-jnp.arange(HALF_HEAD, dtype=F32) / HALF_HEAD)
    ang = jnp.arange(s).astype(F32)[:, None] * freqs[None, :]
    cos, sin = jnp.cos(ang), jnp.sin(ang)
    return jnp.concatenate([cos, cos], axis=1), jnp.concatenate([-sin, sin], axis=1)


def kernel(x, c, mod_w, mod_b, norm_g, mlp_w1, mlp_w2, a_w_in, a_b_in, a_ln_g, a_ln_b, a_w_s, a_b_s, a_w_out, a_b_out, kv_norm_g, kv_mod_w, kv_mod_b, w_kv, cmp_pos_k, cmp_w1_k, cmp_w2_k, cmp_pos_v, cmp_w1_v, cmp_w2_v, b_w_qg, b_w_o, final_g):
    bsz, s, d = x.shape
    depth = mod_w.shape[0]
    n_a = a_w_in.shape[0]
    groups = w_kv.shape[1] // (2 * N_BRANCHES * HEAD_DIM)
    heads = b_w_o.shape[1] // HEAD_DIM
    hg = heads // groups
    assert hg * N_BRANCHES <= LANES

    mod = _cond_matmul(c, mod_w.reshape(depth * 2, d, 3 * d), mod_b.reshape(depth * 2, 3 * d))
    mod = mod.reshape(depth, 2, bsz, 3, 1, d)
    kv_mod = _cond_matmul(c, kv_mod_w[None], kv_mod_b[None]).reshape(bsz, 2, 1, d)
    cos, sin = _rope_tables(s)
    w1_all, w2_all = mlp_w1.astype(BF16), mlp_w2.astype(BF16)

    for layer in range(depth):
        shift, scale, gate = (mod[layer, 0, :, t] for t in range(3))
        if layer < n_a:
            x = _gmlp(x, norm_g[layer, 0], shift, scale, gate,
                      a_w_in[layer].astype(BF16), a_b_in[layer], a_ln_g[layer], a_ln_b[layer],
                      a_w_s[layer], a_b_s[layer], a_w_out[layer].astype(BF16), a_b_out[layer])
        else:
            j = layer - n_a
            if layer == n_a:
                kv_cmp, kv_rest = _kv_proj(x, kv_norm_g, kv_mod[:, 0], kv_mod[:, 1],
                                           w_kv.astype(BF16), cos, sin, groups)
                pos = jnp.stack([cmp_pos_k, cmp_pos_v])
                kv_cmp, kv_cmp_t = _compress(kv_cmp, pos,
                                             jnp.stack([cmp_w1_k, cmp_w1_v]).astype(BF16),
                                             jnp.stack([cmp_w2_k, cmp_w2_v]).astype(BF16))
                v_rest_t = jnp.swapaxes(kv_rest[:, 1::2], 3, 4)
            w_qg = b_w_qg[j]
            w_q = w_qg[:, :heads * HEAD_DIM]
            w_gate = w_qg[:, heads * HEAD_DIM:].reshape(d, groups, hg * N_BRANCHES)
            w_gate = jnp.pad(w_gate, ((0, 0), (0, 0), (0, LANES - hg * N_BRANCHES)))
            w_gate = w_gate.reshape(d, groups * LANES)
            tn = hg * HEAD_DIM
            pad = (-w_gate.shape[1]) % tn
            w_gate = jnp.pad(w_gate, ((0, 0), (0, pad)))
            q, gates = _qg_proj(x, norm_g[layer, 0], shift, scale,
                                w_q.astype(BF16), w_gate.astype(BF16), cos, sin, tn)
            o = _attention(q, gates, kv_cmp, kv_cmp_t, kv_rest, v_rest_t, hg)
            x = _mm_residual(o, b_w_o[j].astype(BF16), x, gate)

        shift, scale, gate = (mod[layer, 1, :, t] for t in range(3))
        x = _mlp(x, norm_g[layer, 1], shift, scale, gate, w1_all, w2_all, layer,
                 final_g, final_norm=(layer == depth - 1))
    return x
```

```python
import functools

import numpy as np
import jax
import jax.numpy as jnp
from jax import lax
from jax.experimental import pallas as pl
from jax.experimental.pallas import tpu as pltpu

BF16 = jnp.bfloat16
F32 = jnp.float32

LANES = 128
SUBLANES = 8
MXU_COLS = 256
HEAD_DIM = 128
HALF_HEAD = HEAD_DIM // 2
N_BRANCHES = 3
CMP_BLOCK = 32
CMP_STRIDE = 16
SEL_BLOCK = 64
N_SELECT = 8
WINDOW = 512
ROPE_THETA = 10000.0
EPS = 1e-6
NEG_INF = -1e30
FORCE_BONUS = 1e6
LOG2_E = 1.4426950408889634
ATTN_TILE = 256
SLC_TILE = 512
VMEM_LIMIT = 56 * 1024 * 1024
MLP_VMEM_LIMIT = 60 * 1024 * 1024


def _dot(a, b):
    return jnp.dot(a, b, preferred_element_type=F32)


def _dot_nt(a, b):
    return lax.dot_general(a, b, (((1,), (1,)), ((), ())), preferred_element_type=F32)


def _tile(n, pref):
    return pref if n % pref == 0 else n


def _params(semantics, vmem_limit=VMEM_LIMIT):
    return pltpu.CompilerParams(dimension_semantics=semantics, vmem_limit_bytes=vmem_limit)


def _rms(x, g):
    return (x * lax.rsqrt(jnp.mean(x * x, axis=-1, keepdims=True) + EPS)) * g


def _norm_mod(x, g, shift, scale):
    return _rms(x, g) * (1.0 + scale) + shift


def _rope(a, cos, sin):
    return a * cos + pltpu.roll(a, HALF_HEAD, 1) * sin


def _dot_chunks(h_ref, w_ref, epilogue, chunk=MXU_COLS):
    n = w_ref.shape[1]
    chunk = chunk if n % chunk == 0 else n
    for c0 in range(0, n, chunk):
        cols = slice(c0, c0 + chunk)
        epilogue(cols, _dot(h_ref[...], w_ref[:, cols]))


def _cond_kernel(c_ref, w_ref, b_ref, o_ref):
    cond = jax.nn.silu(c_ref[...]).astype(BF16)
    o_ref[...] = _dot(cond, w_ref[...].astype(BF16)) + b_ref[...]


def _cond_matmul(c, w, b):
    n, d, e = w.shape
    bsz = c.shape[0]
    te = _tile(e, 1024)
    return pl.pallas_call(
        _cond_kernel,
        grid=(n, e // te),
        in_specs=[
            pl.BlockSpec((bsz, d), lambda s, j: (0, 0)),
            pl.BlockSpec((None, d, te), lambda s, j: (s, 0, j)),
            pl.BlockSpec((None, 1, te), lambda s, j: (s, 0, j)),
        ],
        out_specs=pl.BlockSpec((None, bsz, te), lambda s, j: (s, 0, j)),
        out_shape=jax.ShapeDtypeStruct((n, bsz, e), F32),
        compiler_params=_params(("parallel", "parallel")),
        name="cond_matmul",
    )(c, w, b.reshape(n, 1, e))


def _gmlp_kernel(x_ref, g_ref, sh_ref, sc_ref, gate_ref, win_ref, bin_ref, lg_ref, lb_ref, ws_ref,
                 bs_ref, wout_ref, bout_ref, o_ref, h_scr, v_scr, mix_scr, mu_scr, rs_scr,
                 *, n_half, gdim, chunk):
    j = pl.program_id(2)
    tm = v_scr.shape[0]
    tn = win_ref.shape[1]
    tn_out = wout_ref.shape[1]

    @pl.when(j == 0)
    def _():
        h_scr[...] = _norm_mod(x_ref[...], g_ref[...], sh_ref[...], sc_ref[...]).astype(BF16)

    @pl.when(j < n_half)
    def _():
        base = pl.multiple_of(j * tn, tn)

        def store(cols, acc):
            v_scr[:, pl.ds(base + cols.start, cols.stop - cols.start)] = jax.nn.gelu(acc + bin_ref[:, cols])
        _dot_chunks(h_scr, win_ref, store)

    @pl.when(j == n_half - 1)
    def _():
        v = v_scr[...]
        mu = jnp.mean(v, axis=-1, keepdims=True)
        dv = v - mu
        mu_scr[...] = mu
        rs_scr[...] = lax.rsqrt(jnp.mean(dv * dv, axis=-1, keepdims=True) + EPS)

    @pl.when((j >= n_half) & (j < 2 * n_half))
    def _():
        jb = j - n_half
        base = pl.multiple_of(jb * tn, tn)
        causal = (lax.broadcasted_iota(jnp.int32, (chunk, chunk), 1)
                  <= lax.broadcasted_iota(jnp.int32, (chunk, chunk), 0))

        def store(cols, acc):
            u = jax.nn.gelu(acc + bin_ref[:, cols])
            vcols = pl.ds(base + cols.start, gdim)
            vn = (((v_scr[:, vcols] - mu_scr[...]) * rs_scr[...]) * lg_ref[:, cols]
                  + lb_ref[:, cols]).astype(BF16)
            gi = jb * (tn // gdim) + cols.start // gdim
            w = jnp.where(causal, ws_ref[gi], 0.0).astype(BF16)
            for r0 in range(0, tm, chunk):
                rows = slice(r0, r0 + chunk)
                mixed = _dot(w, vn[rows]) + bs_ref[gi]
                mix_scr[rows, vcols] = (u[rows] * mixed).astype(BF16)
        _dot_chunks(h_scr, win_ref, store, chunk=gdim)

    @pl.when(j >= 2 * n_half)
    def _():
        base = pl.multiple_of((j - 2 * n_half) * tn_out, tn_out)

        def store(cols, acc):
            xcols = pl.ds(base + cols.start, cols.stop - cols.start)
            o_ref[:, cols] = x_ref[:, xcols] + gate_ref[:, cols] * (acc + bout_ref[:, cols])
        _dot_chunks(mix_scr, wout_ref, store)


def _gmlp(x, g, shift, scale, gate, w_in, b_in, ln_g, ln_b, w_s, b_s, w_out, b_out):
    bsz, s, d = x.shape
    width = w_in.shape[1] // 2
    groups, chunk, _ = w_s.shape
    gdim = width // groups
    tm = _tile(s, 512)
    tn = _tile(width, 1024)
    tn_out = _tile(d, 512)
    n_half = width // tn
    n_out = d // tn_out
    assert tn % gdim == 0 and gdim % LANES == 0 and tm % chunk == 0

    def in_col(j):
        return jnp.where(j < n_half, j + n_half, jnp.minimum(j - n_half, n_half - 1))

    def u_col(j):
        return jnp.clip(j - n_half, 0, n_half - 1)

    def out_col(j):
        return jnp.clip(j - 2 * n_half, 0, n_out - 1)

    row = lambda bi, i, j: (bi, 0, 0)
    return pl.pallas_call(
        functools.partial(_gmlp_kernel, n_half=n_half, gdim=gdim, chunk=chunk),
        grid=(bsz, s // tm, 2 * n_half + n_out),
        in_specs=[
            pl.BlockSpec((None, tm, d), lambda bi, i, j: (bi, i, 0)),
            pl.BlockSpec((1, d), lambda bi, i, j: (0, 0)),
            pl.BlockSpec((None, 1, d), row),
            pl.BlockSpec((None, 1, d), row),
            pl.BlockSpec((None, 1, tn_out), lambda bi, i, j: (bi, 0, out_col(j))),
            pl.BlockSpec((d, tn), lambda bi, i, j: (0, in_col(j))),
            pl.BlockSpec((1, tn), lambda bi, i, j: (0, in_col(j))),
            pl.BlockSpec((1, tn), lambda bi, i, j: (0, u_col(j))),
            pl.BlockSpec((1, tn), lambda bi, i, j: (0, u_col(j))),
            pl.BlockSpec((groups, chunk, chunk), lambda bi, i, j: (0, 0, 0)),
            pl.BlockSpec((groups, chunk, 1), lambda bi, i, j: (0, 0, 0)),
            pl.BlockSpec((width, tn_out), lambda bi, i, j: (0, out_col(j))),
            pl.BlockSpec((1, tn_out), lambda bi, i, j: (0, out_col(j))),
        ],
        out_specs=pl.BlockSpec((None, tm, tn_out), lambda bi, i, j: (bi, i, out_col(j))),
        out_shape=jax.ShapeDtypeStruct((bsz, s, d), F32),
        scratch_shapes=[pltpu.VMEM((tm, d), BF16),
                        pltpu.VMEM((tm, width), F32),
                        pltpu.VMEM((tm, width), BF16),
                        pltpu.VMEM((tm, 1), F32), pltpu.VMEM((tm, 1), F32)],
        compiler_params=_params(("parallel", "parallel", "arbitrary")),
        name="gmlp",
    )(x, g.reshape(1, d), shift, scale, gate, w_in, b_in.reshape(1, 2 * width),
      ln_g.reshape(1, width), ln_b.reshape(1, width), w_s, b_s.reshape(groups, chunk, 1),
      w_out, b_out.reshape(1, d))


def _mm_res_kernel(a_ref, w_ref, x_ref, gate_ref, o_ref):
    def store(cols, acc):
        o_ref[:, cols] = x_ref[:, cols] + gate_ref[:, cols] * acc
    _dot_chunks(a_ref, w_ref, store)


def _mm_residual(a, w, x, gate):
    bsz, s, k = a.shape
    d = w.shape[1]
    tm = _tile(s, 1024)
    tn = _tile(d, 512)
    return pl.pallas_call(
        _mm_res_kernel,
        grid=(bsz, s // tm, d // tn),
        in_specs=[
            pl.BlockSpec((None, tm, k), lambda bi, i, j: (bi, i, 0)),
            pl.BlockSpec((k, tn), lambda bi, i, j: (0, j)),
            pl.BlockSpec((None, tm, tn), lambda bi, i, j: (bi, i, j)),
            pl.BlockSpec((None, 1, tn), lambda bi, i, j: (bi, 0, j)),
        ],
        out_specs=pl.BlockSpec((None, tm, tn), lambda bi, i, j: (bi, i, j)),
        out_shape=jax.ShapeDtypeStruct((bsz, s, d), F32),
        compiler_params=_params(("parallel", "parallel", "parallel")),
        name="matmul_residual",
    )(a, w, x, gate)


def _mlp_kernel(xc_ref, xn_ref, g_ref, shn_ref, scn_ref, gate_ref, w1_ref, w2_ref, fg_ref, o_ref,
                h0_scr, h1_scr, acc_scr, *, final_norm):
    u = pl.program_id(0)
    f = pl.program_id(1)
    last = pl.num_programs(1) - 1
    n_tiles = pl.num_programs(0) - 1

    def step(h_ref):
        a = jnp.square(jnp.maximum(_dot(h_ref[...], w1_ref[...]), 0.0)).astype(BF16)
        acc_scr[...] += _dot(a, w2_ref[...])

    def finish():
        xo = xc_ref[...] + gate_ref[...] * acc_scr[...]
        if final_norm:
            xo = _rms(xo, fg_ref[...])
        o_ref[...] = xo
        acc_scr[...] = jnp.zeros_like(acc_scr)

    def prepare(h_ref):
        h_ref[...] = _norm_mod(xn_ref[...], g_ref[...], shn_ref[...], scn_ref[...]).astype(BF16)

    for parity, (h_cur, h_nxt) in enumerate(((h1_scr, h0_scr), (h0_scr, h1_scr))):
        mine = (u % 2 == parity) & (u > 0)

        @pl.when(mine & (f < last))
        def _():
            step(h_cur)

        @pl.when(mine & (u < n_tiles) & (f == last))
        def _():
            step(h_cur)
            finish()
            prepare(h_nxt)

        @pl.when(mine & (u == n_tiles) & (f == last))
        def _():
            step(h_cur)
            finish()

    @pl.when((u == 0) & (f == last))
    def _():
        acc_scr[...] = jnp.zeros_like(acc_scr)
        prepare(h0_scr)


def _mlp(x, g, shift, scale, gate, w1, w2, layer, final_g, final_norm):
    bsz, s, d = x.shape
    dff = w1.shape[2]
    tm = _tile(s, 512)
    tf = _tile(dff, 1024)
    per_batch = s // tm
    n_tiles = bsz * per_batch
    cur = lambda u: jnp.maximum(u - 1, 0)
    nxt = lambda u: jnp.minimum(u, n_tiles - 1)
    vec = pl.BlockSpec((1, d), lambda u, f: (0, 0))
    out = pl.pallas_call(
        functools.partial(_mlp_kernel, final_norm=final_norm),
        grid=(n_tiles + 1, dff // tf),
        in_specs=[
            pl.BlockSpec((None, tm, d), lambda u, f: (cur(u), 0, 0)),
            pl.BlockSpec((None, tm, d), lambda u, f: (nxt(u), 0, 0)),
            vec,
            pl.BlockSpec((None, 1, d), lambda u, f: (nxt(u) // per_batch, 0, 0)),
            pl.BlockSpec((None, 1, d), lambda u, f: (nxt(u) // per_batch, 0, 0)),
            pl.BlockSpec((None, 1, d), lambda u, f: (cur(u) // per_batch, 0, 0)),
            pl.BlockSpec((None, d, tf), lambda u, f: (layer, 0, f)),
            pl.BlockSpec((None, tf, d), lambda u, f: (layer, f, 0)),
            vec,
        ],
        out_specs=pl.BlockSpec((None, tm, d), lambda u, f: (cur(u), 0, 0)),
        out_shape=jax.ShapeDtypeStruct((n_tiles, tm, d), F32),
        scratch_shapes=[pltpu.VMEM((tm, d), BF16), pltpu.VMEM((tm, d), BF16), pltpu.VMEM((tm, d), F32)],
        compiler_params=_params(("arbitrary", "arbitrary"), MLP_VMEM_LIMIT),
        name="mlp",
    )(x.reshape(n_tiles, tm, d), x.reshape(n_tiles, tm, d), g.reshape(1, d), shift, scale, gate,
      w1, w2, final_g.reshape(1, d))
    return out.reshape(bsz, s, d)


def _kv_kernel(x_ref, g_ref, sh_ref, sc_ref, w_ref, cos_ref, sin_ref, cmp_ref, rest_ref, h_scr,
               *, groups):
    j = pl.program_id(2)

    @pl.when(j == 0)
    def _():
        h_scr[...] = _norm_mod(x_ref[...], g_ref[...], sh_ref[...], sc_ref[...]).astype(BF16)

    def emit(dst, rotate):
        def store(cols, acc):
            for c0 in range(0, acc.shape[1], HEAD_DIM):
                a = acc[:, c0:c0 + HEAD_DIM]
                if rotate:
                    a = _rope(a, cos_ref[...], sin_ref[...])
                dst[(cols.start + c0) // HEAD_DIM] = a.astype(dst.dtype)
        _dot_chunks(h_scr, w_ref, store)

    @pl.when(j == 0)
    def _():
        emit(cmp_ref, True)

    @pl.when(j == 1)
    def _():
        emit(cmp_ref, False)

    @pl.when((j >= 2) & (j % 2 == 0))
    def _():
        emit(rest_ref, True)

    @pl.when((j >= 2) & (j % 2 == 1))
    def _():
        emit(rest_ref, False)


def _kv_proj(x, g, shift, scale, w, cos, sin, groups):
    bsz, s, d = x.shape
    tm = _tile(s, 1024)
    tn = groups * HEAD_DIM
    row = lambda bi, i, j: (bi, 0, 0)
    tab = pl.BlockSpec((tm, HEAD_DIM), lambda bi, i, j: (i, 0))
    return pl.pallas_call(
        functools.partial(_kv_kernel, groups=groups),
        grid=(bsz, s // tm, 2 * N_BRANCHES),
        in_specs=[
            pl.BlockSpec((None, tm, d), lambda bi, i, j: (bi, i, 0)),
            pl.BlockSpec((1, d), lambda bi, i, j: (0, 0)),
            pl.BlockSpec((None, 1, d), row), pl.BlockSpec((None, 1, d), row),
            pl.BlockSpec((d, tn), lambda bi, i, j: (0, j)),
            tab, tab,
        ],
        out_specs=[
            pl.BlockSpec((None, None, groups, tm, HEAD_DIM),
                         lambda bi, i, j: (bi, jnp.minimum(j, 1), 0, i, 0)),
            pl.BlockSpec((None, None, groups, tm, HEAD_DIM),
                         lambda bi, i, j: (bi, jnp.maximum(j - 2, 0), 0, i, 0)),
        ],
        out_shape=[jax.ShapeDtypeStruct((bsz, 2, groups, s, HEAD_DIM), F32),
                   jax.ShapeDtypeStruct((bsz, 4, groups, s, HEAD_DIM), BF16)],
        scratch_shapes=[pltpu.VMEM((tm, d), BF16)],
        compiler_params=_params(("parallel", "parallel", "arbitrary")),
        name="kv_proj",
    )(x, g.reshape(1, d), shift, scale, w, cos, sin)


def _qg_kernel(x_ref, g_ref, sh_ref, sc_ref, w_ref, cos_ref, sin_ref, q_ref, gate_ref, h_scr,
               *, heads_per_tile, n_q):
    j = pl.program_id(2)

    @pl.when(j == 0)
    def _():
        h_scr[...] = _norm_mod(x_ref[...], g_ref[...], sh_ref[...], sc_ref[...]).astype(BF16)

    @pl.when(j < n_q)
    def _():
        def store(cols, acc):
            for c0 in range(0, acc.shape[1], HEAD_DIM):
                a = _rope(acc[:, c0:c0 + HEAD_DIM], cos_ref[...], sin_ref[...]) * (HEAD_DIM ** -0.5)
                q_ref[:, cols.start + c0:cols.start + c0 + HEAD_DIM] = a.astype(BF16)
        _dot_chunks(h_scr, w_ref, store)

    @pl.when(j >= n_q)
    def _():
        def store(cols, acc):
            gate_ref[:, cols] = jax.nn.sigmoid(acc)
        _dot_chunks(h_scr, w_ref, store)


def _qg_proj(x, g, shift, scale, w_q, w_gate, cos, sin, tn):
    bsz, s, d = x.shape
    nq_cols, ng_cols = w_q.shape[1], w_gate.shape[1]
    n_q, n_g = nq_cols // tn, ng_cols // tn
    tm = _tile(s, 1024)
    row = lambda bi, i, j: (bi, 0, 0)
    tab = pl.BlockSpec((tm, HEAD_DIM), lambda bi, i, j: (i, 0))
    w = jnp.concatenate([w_q, w_gate], axis=1)
    return pl.pallas_call(
        functools.partial(_qg_kernel, heads_per_tile=tn // HEAD_DIM, n_q=n_q),
        grid=(bsz, s // tm, n_q + n_g),
        in_specs=[
            pl.BlockSpec((None, tm, d), lambda bi, i, j: (bi, i, 0)),
            pl.BlockSpec((1, d), lambda bi, i, j: (0, 0)),
            pl.BlockSpec((None, 1, d), row), pl.BlockSpec((None, 1, d), row),
            pl.BlockSpec((d, tn), lambda bi, i, j: (0, j)),
            tab, tab,
        ],
        out_specs=[
            pl.BlockSpec((None, tm, tn), lambda bi, i, j: (bi, i, jnp.minimum(j, n_q - 1))),
            pl.BlockSpec((None, tm, tn), lambda bi, i, j: (bi, i, jnp.maximum(j - n_q, 0))),
        ],
        out_shape=[jax.ShapeDtypeStruct((bsz, s, nq_cols), BF16),
                   jax.ShapeDtypeStruct((bsz, s, ng_cols), F32)],
        scratch_shapes=[pltpu.VMEM((tm, d), BF16)],
        compiler_params=_params(("parallel", "parallel", "arbitrary")),
        name="qg_proj",
    )(x, g.reshape(1, d), shift, scale, w, cos, sin)


def _compress_kernel(kv_ref, pos_ref, w1_ref, w2_ref, w2t_ref, o_ref, ot_ref):
    n_rows = kv_ref.shape[0] // CMP_STRIDE

    def half_block(p0):
        total = None
        for p in range(0, CMP_STRIDE, 2):
            a = jnp.concatenate(
                [kv_ref[pl.ds(p + q, n_rows, stride=CMP_STRIDE), :] + pos_ref[p0 + p + q:p0 + p + q + 1, :]
                 for q in range(2)], axis=1).astype(BF16)
            part = _dot(a, w1_ref[(p0 + p) * HEAD_DIM:(p0 + p + 2) * HEAD_DIM, :])
            total = part if total is None else total + part
        return total

    first = half_block(0)
    second = half_block(CMP_STRIDE)
    pre = first + pltpu.roll(second, n_rows - 1, 0)
    hidden = jax.nn.gelu(pre).astype(BF16)
    o_ref[...] = _dot(hidden, w2_ref[...]).astype(BF16)
    ot_ref[...] = _dot_nt(w2t_ref[...], hidden).astype(BF16)


def _compress(kv_cmp, pos, w1, w2):
    bsz, _, groups, s, _ = kv_cmp.shape
    n_rows = s // CMP_STRIDE
    hidden = w1.shape[2]
    return pl.pallas_call(
        _compress_kernel,
        grid=(bsz, 2, groups),
        in_specs=[
            pl.BlockSpec((None, None, None, s, HEAD_DIM), lambda bi, t, gi: (bi, t, gi, 0, 0)),
            pl.BlockSpec((None, CMP_BLOCK, HEAD_DIM), lambda bi, t, gi: (t, 0, 0)),
            pl.BlockSpec((None, CMP_BLOCK * HEAD_DIM, hidden), lambda bi, t, gi: (t, 0, 0)),
            pl.BlockSpec((None, hidden, HEAD_DIM), lambda bi, t, gi: (t, 0, 0)),
            pl.BlockSpec((None, HEAD_DIM, hidden), lambda bi, t, gi: (t, 0, 0)),
        ],
        out_specs=[pl.BlockSpec((None, None, None, n_rows, HEAD_DIM), lambda bi, t, gi: (bi, t, gi, 0, 0)),
                   pl.BlockSpec((None, None, None, HEAD_DIM, n_rows), lambda bi, t, gi: (bi, t, gi, 0, 0))],
        out_shape=[jax.ShapeDtypeStruct((bsz, 2, groups, n_rows, HEAD_DIM), BF16),
                   jax.ShapeDtypeStruct((bsz, 2, groups, HEAD_DIM, n_rows), BF16)],
        compiler_params=_params(("parallel", "parallel", "parallel")),
        name="compress",
    )(kv_cmp, pos, w1, w2, jnp.swapaxes(w2, 1, 2))


def _masked_softmax(s, mask, axis):
    s = jnp.where(mask, s, NEG_INF)
    m = jnp.max(s, axis=axis, keepdims=True)
    e = jnp.where(mask, jnp.exp(s - m), 0.0)
    den = jnp.sum(e, axis=axis, keepdims=True)
    return e / jnp.maximum(den, 1e-30)


def _attn_kernel(q_ref, gate_ref, kc_ref, vct_ref, ks_ref, vst_ref, kw_ref, vwt_ref, ovt_ref,
                 o_ref, q4_scr, sel_scr, s_scr, m_scr, acc_scr, oc_scr, ow_scr,
                 *, hg, n_cmp, n_slc, n_sel):
    tq = ATTN_TILE
    tk = SLC_TILE
    i = pl.program_id(2)
    t0 = i * tq
    for h in range(hg):
        q4_scr[h * tq:(h + 1) * tq, :] = q_ref[:, h * HEAD_DIM:(h + 1) * HEAD_DIM]
    q4 = q4_scr[...]
    tpos_l = t0 + lax.broadcasted_iota(jnp.int32, (1, tq), 1)
    tpos4_l = jnp.concatenate([tpos_l] * hg, axis=1)

    def heads(a, h):
        return a[:, h * tq:(h + 1) * tq]

    n_idx = lax.broadcasted_iota(jnp.int32, (kc_ref.shape[0], 1), 0)
    valid = (n_idx * CMP_STRIDE + (CMP_BLOCK - 1) <= tpos4_l) & (n_idx < n_cmp)
    p_cmp = _masked_softmax(_dot_nt(kc_ref[...], q4), valid, 0).astype(BF16)
    oc_scr[...] = _dot(vct_ref[...], p_cmp)

    p_heads = jnp.concatenate([heads(p_cmp, h) for h in range(hg)], axis=0)
    imp = _dot(ovt_ref[...], p_heads)[:n_slc]
    blk = lax.broadcasted_iota(jnp.int32, (n_slc, 1), 0)
    cur = tpos_l // SEL_BLOCK
    forced = (blk == 0) | (blk == cur) | (blk == cur - 1)
    imp = jnp.where(blk * SEL_BLOCK <= tpos_l, imp + jnp.where(forced, FORCE_BONUS, 0.0), NEG_INF)
    rank = jnp.zeros((n_slc, tq), jnp.int32)
    for c in range(n_slc):
        row = imp[c:c + 1, :]
        beats = (row > imp) | ((row == imp) & (c < blk))
        rank = rank + jnp.where(beats, 1, 0)
    sel_scr[...] = jnp.where(rank < n_sel, 1.0, 0.0)

    def with_ones(vt):
        return jnp.concatenate([vt, jnp.ones(vt.shape, BF16)], axis=0)

    def normalised(acc):
        return acc[:HEAD_DIM] / jnp.maximum(acc[HEAD_DIM:], 1e-30)

    def sublane_tile_max(sh):
        return jnp.max(sh.reshape(sh.shape[0] // SUBLANES, SUBLANES, sh.shape[1]), axis=0)

    wk = WINDOW + tq
    kw0 = pl.multiple_of(jnp.maximum(t0 - WINDOW, 0), tq)
    kpos = kw0 + lax.broadcasted_iota(jnp.int32, (wk, 1), 0)
    bias = jnp.where((kpos <= tpos_l) & (kpos > tpos_l - WINDOW), 0.0, NEG_INF)
    s4 = _dot_nt(kw_ref[pl.ds(kw0, wk), :], q4)
    e = []
    for h in range(hg):
        sh = heads(s4, h) + bias
        m = jnp.max(sublane_tile_max(sh), axis=0, keepdims=True)
        e.append(jnp.exp(sh - m).astype(BF16))
    ow_scr[...] = normalised(_dot(with_ones(vwt_ref[:, pl.ds(kw0, wk)]), jnp.concatenate(e, axis=1)))

    n_tiles = i // (tk // tq) + 1
    blocks_per_tile = tk // SEL_BLOCK
    m_scr[...] = jnp.full_like(m_scr, NEG_INF)

    def scores(kt, carry):
        k0 = pl.multiple_of(kt * tk, tk)
        sel = sel_scr[pl.ds(pl.multiple_of(kt * blocks_per_tile, blocks_per_tile), blocks_per_tile), :]
        bias = []
        for jb in range(blocks_per_tile):
            kpos = k0 + jb * SEL_BLOCK + lax.broadcasted_iota(jnp.int32, (SEL_BLOCK, 1), 0)
            bias.append(jnp.where((sel[jb:jb + 1, :] > 0.5) & (kpos <= tpos_l), 0.0, NEG_INF))
        bias = jnp.concatenate(bias, axis=0)
        s4 = _dot_nt(ks_ref[pl.ds(k0, tk), :], q4)
        for h in range(hg):
            cols = slice(h * tq, (h + 1) * tq)
            sh = (heads(s4, h) + bias) * LOG2_E
            s_scr[pl.ds(k0, tk), cols] = sh
            m_scr[:, cols] = jnp.maximum(m_scr[:, cols], sublane_tile_max(sh))
        return carry

    lax.fori_loop(0, n_tiles, scores, 0)
    m_scr[...] = jnp.broadcast_to(jnp.max(m_scr[...], axis=0, keepdims=True), m_scr.shape)
    acc_scr[...] = jnp.zeros_like(acc_scr)

    def weighted(kt, carry):
        k0 = pl.multiple_of(kt * tk, tk)
        s = s_scr[pl.ds(k0, tk), :].reshape(tk // SUBLANES, SUBLANES, hg * tq)
        e = jnp.exp2(s - m_scr[...]).reshape(tk, hg * tq).astype(BF16)
        acc_scr[...] += _dot(with_ones(vst_ref[:, pl.ds(k0, tk)]), e)
        return carry

    lax.fori_loop(0, n_tiles, weighted, 0)

    o_slc = normalised(acc_scr[...])
    gates_t = gate_ref[...].T
    for h in range(hg):
        c0 = h * N_BRANCHES
        o = (gates_t[c0:c0 + 1] * heads(oc_scr, h) + gates_t[c0 + 1:c0 + 2] * heads(o_slc, h)
             + gates_t[c0 + 2:c0 + 3] * heads(ow_scr, h))
        o_ref[:, h * HEAD_DIM:(h + 1) * HEAD_DIM] = o.T.astype(BF16)


def _attention(q, gates, kv_cmp, kv_cmp_t, kv_rest, v_rest_t, hg):
    bsz, s, _ = q.shape
    groups = kv_rest.shape[2]
    n_rows = kv_cmp.shape[3]
    n_cmp = n_rows - 1
    n_slc = s // SEL_BLOCK
    n_sel = min(N_SELECT, n_slc)
    tq = ATTN_TILE
    assert n_rows == LANES and n_slc <= LANES and n_slc % (SLC_TILE // SEL_BLOCK) == 0
    assert s % SLC_TILE == 0 and s >= WINDOW + tq

    ci = np.arange(LANES)[None, :]
    sj = np.arange(LANES)[:, None]
    overlap_t = ((ci * CMP_STRIDE <= sj * SEL_BLOCK + SEL_BLOCK - 1)
                 & (ci * CMP_STRIDE + CMP_BLOCK - 1 >= sj * SEL_BLOCK) & (ci < n_cmp) & (sj < n_slc))
    overlap_t = jnp.asarray(np.tile(overlap_t, (1, hg)), BF16)

    def k_spec(which):
        return pl.BlockSpec((None, None, None, s, HEAD_DIM), lambda bi, gi, i: (bi, which, gi, 0, 0))

    def vt_spec(which):
        return pl.BlockSpec((None, None, None, HEAD_DIM, s), lambda bi, gi, i: (bi, which, gi, 0, 0))

    def cmp_spec(which):
        return pl.BlockSpec((None, None, None, n_rows, HEAD_DIM), lambda bi, gi, i: (bi, which, gi, 0, 0))

    qblk = pl.BlockSpec((None, tq, hg * HEAD_DIM), lambda bi, gi, i: (bi, i, gi))
    return pl.pallas_call(
        functools.partial(_attn_kernel, hg=hg, n_cmp=n_cmp, n_slc=n_slc, n_sel=n_sel),
        grid=(bsz, groups, s // tq),
        in_specs=[
            qblk,
            pl.BlockSpec((None, tq, LANES), lambda bi, gi, i: (bi, i, gi)),
            cmp_spec(0), cmp_spec(1),
            k_spec(0), vt_spec(0), k_spec(2), vt_spec(1),
            pl.BlockSpec((LANES, hg * LANES), lambda bi, gi, i: (0, 0)),
        ],
        out_specs=qblk,
        out_shape=jax.ShapeDtypeStruct(q.shape, BF16),
        scratch_shapes=[pltpu.VMEM((hg * tq, HEAD_DIM), BF16),
                        pltpu.VMEM((n_slc, tq), F32),
                        pltpu.VMEM((s, hg * tq), F32),
                        pltpu.VMEM((SUBLANES, hg * tq), F32),
                        pltpu.VMEM((2 * HEAD_DIM, hg * tq), F32),
                        pltpu.VMEM((HEAD_DIM, hg * tq), F32),
                        pltpu.VMEM((HEAD_DIM, hg * tq), F32)],
        compiler_params=_params(("parallel", "parallel", "arbitrary")),
        name="nsa_attention",
    )(q, gates, kv_cmp, kv_cmp_t, kv_rest, v_rest_t, kv_rest, v_rest_t, overlap_t)


def _rope_tables(s):
    freqs = ROPE_THETA ** (-jnp.arange(HALF_HEAD, dtype=F32) / HALF_HEAD)
    ang = jnp.arange(s).astype(F32)[:, None] * freqs[None, :]
    cos, sin = jnp.cos(ang), jnp.sin(ang)
    return jnp.concatenate([cos, cos], axis=1), jnp.concatenate([-sin, sin], axis=1)


def kernel(x, c, mod_w, mod_b, norm_g, mlp_w1, mlp_w2, a_w_in, a_b_in, a_ln_g, a_ln_b, a_w_s, a_b_s, a_w_out, a_b_out, kv_norm_g, kv_mod_w, kv_mod_b, w_kv, cmp_pos_k, cmp_w1_k, cmp_w2_k, cmp_pos_v, cmp_w1_v, cmp_w2_v, b_w_qg, b_w_o, final_g):
    bsz, s, d = x.shape
    depth = mod_w.shape[0]
    n_a = a_w_in.shape[0]
    groups = w_kv.shape[1] // (2 * N_BRANCHES * HEAD_DIM)
    heads = b_w_o.shape[1] // HEAD_DIM
    hg = heads // groups
    assert hg * N_BRANCHES <= LANES

    mod = _cond_matmul(c, mod_w.reshape(depth * 2, d, 3 * d), mod_b.reshape(depth * 2, 3 * d))
    mod = mod.reshape(depth, 2, bsz, 3, 1, d)
    kv_mod = _cond_matmul(c, kv_mod_w[None], kv_mod_b[None]).reshape(bsz, 2, 1, d)
    cos, sin = _rope_tables(s)
    w1_all, w2_all = mlp_w1.astype(BF16), mlp_w2.astype(BF16)

    for layer in range(depth):
        shift, scale, gate = (mod[layer, 0, :, t] for t in range(3))
        if layer < n_a:
            x = _gmlp(x, norm_g[layer, 0], shift, scale, gate,
                      a_w_in[layer].astype(BF16), a_b_in[layer], a_ln_g[layer], a_ln_b[layer],
                      a_w_s[layer], a_b_s[layer], a_w_out[layer].astype(BF16), a_b_out[layer])
        else:
            j = layer - n_a
            if layer == n_a:
                kv_cmp, kv_rest = _kv_proj(x, kv_norm_g, kv_mod[:, 0], kv_mod[:, 1],
                                           w_kv.astype(BF16), cos, sin, groups)
                pos = jnp.stack([cmp_pos_k, cmp_pos_v])
                kv_cmp, kv_cmp_t = _compress(kv_cmp, pos,
                                             jnp.stack([cmp_w1_k, cmp_w1_v]).astype(BF16),
                                             jnp.stack([cmp_w2_k, cmp_w2_v]).astype(BF16))
                v_rest_t = jnp.swapaxes(kv_rest[:, 1::2], 3, 4)
            w_qg = b_w_qg[j]
            w_q = w_qg[:, :heads * HEAD_DIM]
            w_gate = w_qg[:, heads * HEAD_DIM:].reshape(d, groups, hg * N_BRANCHES)
            w_gate = jnp.pad(w_gate, ((0, 0), (0, 0), (0, LANES - hg * N_BRANCHES)))
            w_gate = w_gate.reshape(d, groups * LANES)
            tn = hg * HEAD_DIM
            pad = (-w_gate.shape[1]) % tn
            w_gate = jnp.pad(w_gate, ((0, 0), (0, pad)))
            q, gates = _qg_proj(x, norm_g[layer, 0], shift, scale,
                                w_q.astype(BF16), w_gate.astype(BF16), cos, sin, tn)
            o = _attention(q, gates, kv_cmp, kv_cmp_t, kv_rest, v_rest_t, hg)
            x = _mm_residual(o, b_w_o[j].astype(BF16), x, gate)

        shift, scale, gate = (mod[layer, 1, :, t] for t in range(3))
        x = _mlp(x, norm_g[layer, 1], shift, scale, gate, w1_all, w2_all, layer,
                 final_g, final_norm=(layer == depth - 1))
    return x
```

```python
import functools

import numpy as np
import jax
import jax.numpy as jnp
from jax import lax
from jax.experimental import pallas as pl
from jax.experimental.pallas import tpu as pltpu

BF16 = jnp.bfloat16
F32 = jnp.float32

LANES = 128
SUBLANES = 8
MXU_COLS = 256
HEAD_DIM = 128
HALF_HEAD = HEAD_DIM // 2
N_BRANCHES = 3
CMP_BLOCK = 32
CMP_STRIDE = 16
SEL_BLOCK = 64
N_SELECT = 8
WINDOW = 512
ROPE_THETA = 10000.0
EPS = 1e-6
NEG_INF = -1e30
FORCE_BONUS = 1e6
LOG2_E = 1.4426950408889634
ATTN_TILE = 512
SLC_TILE = 512
VMEM_LIMIT = 56 * 1024 * 1024


def _dot(a, b):
    return jnp.dot(a, b, preferred_element_type=F32)


def _dot_nt(a, b):
    return lax.dot_general(a, b, (((1,), (1,)), ((), ())), preferred_element_type=F32)


def _tile(n, pref):
    return pref if n % pref == 0 else n


def _params(semantics):
    return pltpu.CompilerParams(dimension_semantics=semantics, vmem_limit_bytes=VMEM_LIMIT)


def _rms(x, g):
    return (x * lax.rsqrt(jnp.mean(x * x, axis=-1, keepdims=True) + EPS)) * g


def _norm_mod(x, g, shift, scale):
    return _rms(x, g) * (1.0 + scale) + shift


def _rope(a, cos, sin):
    return a * cos + pltpu.roll(a, HALF_HEAD, 1) * sin


def _dot_chunks(h_ref, w_ref, epilogue, chunk=MXU_COLS):
    n = w_ref.shape[1]
    chunk = chunk if n % chunk == 0 else n
    for c0 in range(0, n, chunk):
        cols = slice(c0, c0 + chunk)
        epilogue(cols, _dot(h_ref[...], w_ref[:, cols]))


def _cond_kernel(c_ref, w_ref, b_ref, o_ref):
    cond = jax.nn.silu(c_ref[...]).astype(BF16)
    o_ref[...] = _dot(cond, w_ref[...].astype(BF16)) + b_ref[...]


def _cond_matmul(c, w, b):
    n, d, e = w.shape
    bsz = c.shape[0]
    te = _tile(e, 1024)
    return pl.pallas_call(
        _cond_kernel,
        grid=(n, e // te),
        in_specs=[
            pl.BlockSpec((bsz, d), lambda s, j: (0, 0)),
            pl.BlockSpec((None, d, te), lambda s, j: (s, 0, j)),
            pl.BlockSpec((None, 1, te), lambda s, j: (s, 0, j)),
        ],
        out_specs=pl.BlockSpec((None, bsz, te), lambda s, j: (s, 0, j)),
        out_shape=jax.ShapeDtypeStruct((n, bsz, e), F32),
        compiler_params=_params(("parallel", "parallel")),
        name="cond_matmul",
    )(c, w, b.reshape(n, 1, e))


def _gmlp_kernel(x_ref, g_ref, sh_ref, sc_ref, gate_ref, win_ref, bin_ref, lg_ref, lb_ref, ws_ref,
                 bs_ref, wout_ref, bout_ref, o_ref, h_scr, v_scr, mix_scr, mu_scr, rs_scr,
                 *, n_half, gdim, chunk):
    j = pl.program_id(2)
    tm = v_scr.shape[0]
    tn = win_ref.shape[1]
    tn_out = wout_ref.shape[1]

    @pl.when(j == 0)
    def _():
        h_scr[...] = _norm_mod(x_ref[...], g_ref[...], sh_ref[...], sc_ref[...]).astype(BF16)

    @pl.when(j < n_half)
    def _():
        base = pl.multiple_of(j * tn, tn)

        def store(cols, acc):
            v_scr[:, pl.ds(base + cols.start, cols.stop - cols.start)] = jax.nn.gelu(acc + bin_ref[:, cols])
        _dot_chunks(h_scr, win_ref, store)

    @pl.when(j == n_half - 1)
    def _():
        v = v_scr[...]
        mu = jnp.mean(v, axis=-1, keepdims=True)
        dv = v - mu
        mu_scr[...] = mu
        rs_scr[...] = lax.rsqrt(jnp.mean(dv * dv, axis=-1, keepdims=True) + EPS)

    @pl.when((j >= n_half) & (j < 2 * n_half))
    def _():
        jb = j - n_half
        base = pl.multiple_of(jb * tn, tn)
        causal = (lax.broadcasted_iota(jnp.int32, (chunk, chunk), 1)
                  <= lax.broadcasted_iota(jnp.int32, (chunk, chunk), 0))

        def store(cols, acc):
            u = jax.nn.gelu(acc + bin_ref[:, cols])
            vcols = pl.ds(base + cols.start, gdim)
            vn = (((v_scr[:, vcols] - mu_scr[...]) * rs_scr[...]) * lg_ref[:, cols]
                  + lb_ref[:, cols]).astype(BF16)
            gi = jb * (tn // gdim) + cols.start // gdim
            w = jnp.where(causal, ws_ref[gi], 0.0).astype(BF16)
            for r0 in range(0, tm, chunk):
                rows = slice(r0, r0 + chunk)
                mixed = _dot(w, vn[rows]) + bs_ref[gi]
                mix_scr[rows, vcols] = (u[rows] * mixed).astype(BF16)
        _dot_chunks(h_scr, win_ref, store, chunk=gdim)

    @pl.when(j >= 2 * n_half)
    def _():
        base = pl.multiple_of((j - 2 * n_half) * tn_out, tn_out)

        def store(cols, acc):
            xcols = pl.ds(base + cols.start, cols.stop - cols.start)
            o_ref[:, cols] = x_ref[:, xcols] + gate_ref[:, cols] * (acc + bout_ref[:, cols])
        _dot_chunks(mix_scr, wout_ref, store)


def _gmlp(x, g, shift, scale, gate, w_in, b_in, ln_g, ln_b, w_s, b_s, w_out, b_out):
    bsz, s, d = x.shape
    width = w_in.shape[1] // 2
    groups, chunk, _ = w_s.shape
    gdim = width // groups
    tm = _tile(s, 512)
    tn = _tile(width, 1024)
    tn_out = _tile(d, 512)
    n_half = width // tn
    n_out = d // tn_out
    assert tn % gdim == 0 and gdim % LANES == 0 and tm % chunk == 0

    def in_col(j):
        return jnp.where(j < n_half, j + n_half, jnp.minimum(j - n_half, n_half - 1))

    def u_col(j):
        return jnp.clip(j - n_half, 0, n_half - 1)

    def out_col(j):
        return jnp.clip(j - 2 * n_half, 0, n_out - 1)

    row = lambda bi, i, j: (bi, 0, 0)
    return pl.pallas_call(
        functools.partial(_gmlp_kernel, n_half=n_half, gdim=gdim, chunk=chunk),
        grid=(bsz, s // tm, 2 * n_half + n_out),
        in_specs=[
            pl.BlockSpec((None, tm, d), lambda bi, i, j: (bi, i, 0)),
            pl.BlockSpec((1, d), lambda bi, i, j: (0, 0)),
            pl.BlockSpec((None, 1, d), row),
            pl.BlockSpec((None, 1, d), row),
            pl.BlockSpec((None, 1, tn_out), lambda bi, i, j: (bi, 0, out_col(j))),
            pl.BlockSpec((d, tn), lambda bi, i, j: (0, in_col(j))),
            pl.BlockSpec((1, tn), lambda bi, i, j: (0, in_col(j))),
            pl.BlockSpec((1, tn), lambda bi, i, j: (0, u_col(j))),
            pl.BlockSpec((1, tn), lambda bi, i, j: (0, u_col(j))),
            pl.BlockSpec((groups, chunk, chunk), lambda bi, i, j: (0, 0, 0)),
            pl.BlockSpec((groups, chunk, 1), lambda bi, i, j: (0, 0, 0)),
            pl.BlockSpec((width, tn_out), lambda bi, i, j: (0, out_col(j))),
            pl.BlockSpec((1, tn_out), lambda bi, i, j: (0, out_col(j))),
        ],
        out_specs=pl.BlockSpec((None, tm, tn_out), lambda bi, i, j: (bi, i, out_col(j))),
        out_shape=jax.ShapeDtypeStruct((bsz, s, d), F32),
        scratch_shapes=[pltpu.VMEM((tm, d), BF16),
                        pltpu.VMEM((tm, width), F32),
                        pltpu.VMEM((tm, width), BF16),
                        pltpu.VMEM((tm, 1), F32), pltpu.VMEM((tm, 1), F32)],
        compiler_params=_params(("parallel", "parallel", "arbitrary")),
        name="gmlp",
    )(x, g.reshape(1, d), shift, scale, gate, w_in, b_in.reshape(1, 2 * width),
      ln_g.reshape(1, width), ln_b.reshape(1, width), w_s, b_s.reshape(groups, chunk, 1),
      w_out, b_out.reshape(1, d))


def _mm_res_kernel(a_ref, w_ref, x_ref, gate_ref, o_ref):
    def store(cols, acc):
        o_ref[:, cols] = x_ref[:, cols] + gate_ref[:, cols] * acc
    _dot_chunks(a_ref, w_ref, store)


def _mm_residual(a, w, x, gate):
    bsz, s, k = a.shape
    d = w.shape[1]
    tm = _tile(s, 1024)
    tn = _tile(d, 512)
    return pl.pallas_call(
        _mm_res_kernel,
        grid=(bsz, s // tm, d // tn),
        in_specs=[
            pl.BlockSpec((None, tm, k), lambda bi, i, j: (bi, i, 0)),
            pl.BlockSpec((k, tn), lambda bi, i, j: (0, j)),
            pl.BlockSpec((None, tm, tn), lambda bi, i, j: (bi, i, j)),
            pl.BlockSpec((None, 1, tn), lambda bi, i, j: (bi, 0, j)),
        ],
        out_specs=pl.BlockSpec((None, tm, tn), lambda bi, i, j: (bi, i, j)),
        out_shape=jax.ShapeDtypeStruct((bsz, s, d), F32),
        compiler_params=_params(("parallel", "parallel", "parallel")),
        name="matmul_residual",
    )(a, w, x, gate)


def _mlp_kernel(x_ref, g_ref, sh_ref, sc_ref, gate_ref, w1_ref, w2_ref, fg_ref, o_ref,
                h_scr, acc_scr, *, final_norm):
    f = pl.program_id(2)

    @pl.when(f == 0)
    def _():
        h_scr[...] = _norm_mod(x_ref[...], g_ref[...], sh_ref[...], sc_ref[...]).astype(BF16)
        acc_scr[...] = jnp.zeros_like(acc_scr)

    a = jnp.square(jnp.maximum(_dot(h_scr[...], w1_ref[...]), 0.0)).astype(BF16)
    acc_scr[...] += _dot(a, w2_ref[...])

    @pl.when(f == pl.num_programs(2) - 1)
    def _():
        xn = x_ref[...] + gate_ref[...] * acc_scr[...]
        if final_norm:
            xn = _rms(xn, fg_ref[...])
        o_ref[...] = xn


def _mlp(x, g, shift, scale, gate, w1, w2, layer, final_g, final_norm):
    bsz, s, d = x.shape
    dff = w1.shape[2]
    tm = _tile(s, 512)
    tf = _tile(dff, 1024)
    row = lambda bi, i, f: (bi, 0, 0)
    vec = pl.BlockSpec((1, d), lambda bi, i, f: (0, 0))
    xblk = pl.BlockSpec((None, tm, d), lambda bi, i, f: (bi, i, 0))
    return pl.pallas_call(
        functools.partial(_mlp_kernel, final_norm=final_norm),
        grid=(bsz, s // tm, dff // tf),
        in_specs=[
            xblk, vec,
            pl.BlockSpec((None, 1, d), row), pl.BlockSpec((None, 1, d), row),
            pl.BlockSpec((None, 1, d), row),
            pl.BlockSpec((None, d, tf), lambda bi, i, f: (layer, 0, f)),
            pl.BlockSpec((None, tf, d), lambda bi, i, f: (layer, f, 0)),
            vec,
        ],
        out_specs=xblk,
        out_shape=jax.ShapeDtypeStruct((bsz, s, d), F32),
        scratch_shapes=[pltpu.VMEM((tm, d), BF16), pltpu.VMEM((tm, d), F32)],
        compiler_params=_params(("parallel", "parallel", "arbitrary")),
        name="mlp",
    )(x, g.reshape(1, d), shift, scale, gate, w1, w2, final_g.reshape(1, d))


def _kv_kernel(x_ref, g_ref, sh_ref, sc_ref, w_ref, cos_ref, sin_ref, cmp_ref, rest_ref, h_scr,
               *, groups):
    j = pl.program_id(2)

    @pl.when(j == 0)
    def _():
        h_scr[...] = _norm_mod(x_ref[...], g_ref[...], sh_ref[...], sc_ref[...]).astype(BF16)

    def emit(dst, rotate):
        def store(cols, acc):
            for c0 in range(0, acc.shape[1], HEAD_DIM):
                a = acc[:, c0:c0 + HEAD_DIM]
                if rotate:
                    a = _rope(a, cos_ref[...], sin_ref[...])
                dst[(cols.start + c0) // HEAD_DIM] = a.astype(dst.dtype)
        _dot_chunks(h_scr, w_ref, store)

    @pl.when(j == 0)
    def _():
        emit(cmp_ref, True)

    @pl.when(j == 1)
    def _():
        emit(cmp_ref, False)

    @pl.when((j >= 2) & (j % 2 == 0))
    def _():
        emit(rest_ref, True)

    @pl.when((j >= 2) & (j % 2 == 1))
    def _():
        emit(rest_ref, False)


def _kv_proj(x, g, shift, scale, w, cos, sin, groups):
    bsz, s, d = x.shape
    tm = _tile(s, 1024)
    tn = groups * HEAD_DIM
    row = lambda bi, i, j: (bi, 0, 0)
    tab = pl.BlockSpec((tm, HEAD_DIM), lambda bi, i, j: (i, 0))
    return pl.pallas_call(
        functools.partial(_kv_kernel, groups=groups),
        grid=(bsz, s // tm, 2 * N_BRANCHES),
        in_specs=[
            pl.BlockSpec((None, tm, d), lambda bi, i, j: (bi, i, 0)),
            pl.BlockSpec((1, d), lambda bi, i, j: (0, 0)),
            pl.BlockSpec((None, 1, d), row), pl.BlockSpec((None, 1, d), row),
            pl.BlockSpec((d, tn), lambda bi, i, j: (0, j)),
            tab, tab,
        ],
        out_specs=[
            pl.BlockSpec((None, None, groups, tm, HEAD_DIM),
                         lambda bi, i, j: (bi, jnp.minimum(j, 1), 0, i, 0)),
            pl.BlockSpec((None, None, groups, tm, HEAD_DIM),
                         lambda bi, i, j: (bi, jnp.maximum(j - 2, 0), 0, i, 0)),
        ],
        out_shape=[jax.ShapeDtypeStruct((bsz, 2, groups, s, HEAD_DIM), F32),
                   jax.ShapeDtypeStruct((bsz, 4, groups, s, HEAD_DIM), BF16)],
        scratch_shapes=[pltpu.VMEM((tm, d), BF16)],
        compiler_params=_params(("parallel", "parallel", "arbitrary")),
        name="kv_proj",
    )(x, g.reshape(1, d), shift, scale, w, cos, sin)


def _qg_kernel(x_ref, g_ref, sh_ref, sc_ref, w_ref, cos_ref, sin_ref, q_ref, gate_ref, h_scr,
               *, heads_per_tile, n_q):
    j = pl.program_id(2)

    @pl.when(j == 0)
    def _():
        h_scr[...] = _norm_mod(x_ref[...], g_ref[...], sh_ref[...], sc_ref[...]).astype(BF16)

    @pl.when(j < n_q)
    def _():
        def store(cols, acc):
            for c0 in range(0, acc.shape[1], HEAD_DIM):
                a = _rope(acc[:, c0:c0 + HEAD_DIM], cos_ref[...], sin_ref[...]) * (HEAD_DIM ** -0.5)
                q_ref[:, cols.start + c0:cols.start + c0 + HEAD_DIM] = a.astype(BF16)
        _dot_chunks(h_scr, w_ref, store)

    @pl.when(j >= n_q)
    def _():
        def store(cols, acc):
            gate_ref[:, cols] = jax.nn.sigmoid(acc)
        _dot_chunks(h_scr, w_ref, store)


def _qg_proj(x, g, shift, scale, w_q, w_gate, cos, sin, tn):
    bsz, s, d = x.shape
    nq_cols, ng_cols = w_q.shape[1], w_gate.shape[1]
    n_q, n_g = nq_cols // tn, ng_cols // tn
    tm = _tile(s, 1024)
    row = lambda bi, i, j: (bi, 0, 0)
    tab = pl.BlockSpec((tm, HEAD_DIM), lambda bi, i, j: (i, 0))
    w = jnp.concatenate([w_q, w_gate], axis=1)
    return pl.pallas_call(
        functools.partial(_qg_kernel, heads_per_tile=tn // HEAD_DIM, n_q=n_q),
        grid=(bsz, s // tm, n_q + n_g),
        in_specs=[
            pl.BlockSpec((None, tm, d), lambda bi, i, j: (bi, i, 0)),
            pl.BlockSpec((1, d), lambda bi, i, j: (0, 0)),
            pl.BlockSpec((None, 1, d), row), pl.BlockSpec((None, 1, d), row),
            pl.BlockSpec((d, tn), lambda bi, i, j: (0, j)),
            tab, tab,
        ],
        out_specs=[
            pl.BlockSpec((None, tm, tn), lambda bi, i, j: (bi, i, jnp.minimum(j, n_q - 1))),
            pl.BlockSpec((None, tm, tn), lambda bi, i, j: (bi, i, jnp.maximum(j - n_q, 0))),
        ],
        out_shape=[jax.ShapeDtypeStruct((bsz, s, nq_cols), BF16),
                   jax.ShapeDtypeStruct((bsz, s, ng_cols), F32)],
        scratch_shapes=[pltpu.VMEM((tm, d), BF16)],
        compiler_params=_params(("parallel", "parallel", "arbitrary")),
        name="qg_proj",
    )(x, g.reshape(1, d), shift, scale, w, cos, sin)


def _compress_kernel(kv_ref, pos_ref, w1_ref, w2_ref, w2t_ref, o_ref, ot_ref):
    n_rows = kv_ref.shape[0] // CMP_STRIDE

    def half_block(p0):
        total = None
        for p in range(0, CMP_STRIDE, 2):
            a = jnp.concatenate(
                [kv_ref[pl.ds(p + q, n_rows, stride=CMP_STRIDE), :] + pos_ref[p0 + p + q:p0 + p + q + 1, :]
                 for q in range(2)], axis=1).astype(BF16)
            part = _dot(a, w1_ref[(p0 + p) * HEAD_DIM:(p0 + p + 2) * HEAD_DIM, :])
            total = part if total is None else total + part
        return total

    first = half_block(0)
    second = half_block(CMP_STRIDE)
    pre = first + pltpu.roll(second, n_rows - 1, 0)
    hidden = jax.nn.gelu(pre).astype(BF16)
    o_ref[...] = _dot(hidden, w2_ref[...]).astype(BF16)
    ot_ref[...] = _dot_nt(w2t_ref[...], hidden).astype(BF16)


def _compress(kv_cmp, pos, w1, w2):
    bsz, _, groups, s, _ = kv_cmp.shape
    n_rows = s // CMP_STRIDE
    hidden = w1.shape[2]
    return pl.pallas_call(
        _compress_kernel,
        grid=(bsz, 2, groups),
        in_specs=[
            pl.BlockSpec((None, None, None, s, HEAD_DIM), lambda bi, t, gi: (bi, t, gi, 0, 0)),
            pl.BlockSpec((None, CMP_BLOCK, HEAD_DIM), lambda bi, t, gi: (t, 0, 0)),
            pl.BlockSpec((None, CMP_BLOCK * HEAD_DIM, hidden), lambda bi, t, gi: (t, 0, 0)),
            pl.BlockSpec((None, hidden, HEAD_DIM), lambda bi, t, gi: (t, 0, 0)),
            pl.BlockSpec((None, HEAD_DIM, hidden), lambda bi, t, gi: (t, 0, 0)),
        ],
        out_specs=[pl.BlockSpec((None, None, None, n_rows, HEAD_DIM), lambda bi, t, gi: (bi, t, gi, 0, 0)),
                   pl.BlockSpec((None, None, None, HEAD_DIM, n_rows), lambda bi, t, gi: (bi, t, gi, 0, 0))],
        out_shape=[jax.ShapeDtypeStruct((bsz, 2, groups, n_rows, HEAD_DIM), BF16),
                   jax.ShapeDtypeStruct((bsz, 2, groups, HEAD_DIM, n_rows), BF16)],
        compiler_params=_params(("parallel", "parallel", "parallel")),
        name="compress",
    )(kv_cmp, pos, w1, w2, jnp.swapaxes(w2, 1, 2))


def _masked_softmax(s, mask, axis):
    s = jnp.where(mask, s, NEG_INF)
    m = jnp.max(s, axis=axis, keepdims=True)
    e = jnp.where(mask, jnp.exp(s - m), 0.0)
    den = jnp.sum(e, axis=axis, keepdims=True)
    return e / jnp.maximum(den, 1e-30)


def _attn_kernel(q_ref, gate_ref, kc_ref, vct_ref, ks_ref, vst_ref, kw_ref, vwt_ref, ovt_ref,
                 o_ref, q4_scr, sel_scr, s_scr, m_scr, acc_scr, oc_scr, ow_scr,
                 *, hg, n_cmp, n_slc, n_sel):
    tq = ATTN_TILE
    tk = SLC_TILE
    i = pl.program_id(2)
    t0 = i * tq
    for h in range(hg):
        q4_scr[h * tq:(h + 1) * tq, :] = q_ref[:, h * HEAD_DIM:(h + 1) * HEAD_DIM]
    q4 = q4_scr[...]
    tpos_l = t0 + lax.broadcasted_iota(jnp.int32, (1, tq), 1)
    tpos4_l = jnp.concatenate([tpos_l] * hg, axis=1)

    def heads(a, h):
        return a[:, h * tq:(h + 1) * tq]

    n_idx = lax.broadcasted_iota(jnp.int32, (kc_ref.shape[0], 1), 0)
    valid = (n_idx * CMP_STRIDE + (CMP_BLOCK - 1) <= tpos4_l) & (n_idx < n_cmp)
    p_cmp = _masked_softmax(_dot_nt(kc_ref[...], q4), valid, 0).astype(BF16)
    oc_scr[...] = _dot(vct_ref[...], p_cmp)

    p_heads = jnp.concatenate([heads(p_cmp, h) for h in range(hg)], axis=0)
    imp = _dot(ovt_ref[...], p_heads)[:n_slc]
    blk = lax.broadcasted_iota(jnp.int32, (n_slc, 1), 0)
    cur = tpos_l // SEL_BLOCK
    forced = (blk == 0) | (blk == cur) | (blk == cur - 1)
    imp = jnp.where(blk * SEL_BLOCK <= tpos_l, imp + jnp.where(forced, FORCE_BONUS, 0.0), NEG_INF)
    rank = jnp.zeros((n_slc, tq), jnp.int32)
    for c in range(n_slc):
        row = imp[c:c + 1, :]
        beats = (row > imp) | ((row == imp) & (c < blk))
        rank = rank + jnp.where(beats, 1, 0)
    sel_scr[...] = jnp.where(rank < n_sel, 1.0, 0.0)

    def with_ones(vt):
        return jnp.concatenate([vt, jnp.ones(vt.shape, BF16)], axis=0)

    def normalised(acc):
        return acc[:HEAD_DIM] / jnp.maximum(acc[HEAD_DIM:], 1e-30)

    def sublane_tile_max(sh):
        return jnp.max(sh.reshape(sh.shape[0] // SUBLANES, SUBLANES, sh.shape[1]), axis=0)

    wk = WINDOW + tq
    kw0 = pl.multiple_of(jnp.maximum(t0 - WINDOW, 0), tq)
    kpos = kw0 + lax.broadcasted_iota(jnp.int32, (wk, 1), 0)
    bias = jnp.where((kpos <= tpos_l) & (kpos > tpos_l - WINDOW), 0.0, NEG_INF)
    s4 = _dot_nt(kw_ref[pl.ds(kw0, wk), :], q4)
    e = []
    for h in range(hg):
        sh = heads(s4, h) + bias
        m = jnp.max(sublane_tile_max(sh), axis=0, keepdims=True)
        e.append(jnp.exp(sh - m).astype(BF16))
    ow_scr[...] = normalised(_dot(with_ones(vwt_ref[:, pl.ds(kw0, wk)]), jnp.concatenate(e, axis=1)))

    n_tiles = i // (tk // tq) + 1
    blocks_per_tile = tk // SEL_BLOCK
    m_scr[...] = jnp.full_like(m_scr, NEG_INF)

    def scores(kt, carry):
        k0 = pl.multiple_of(kt * tk, tk)
        sel = sel_scr[pl.ds(pl.multiple_of(kt * blocks_per_tile, blocks_per_tile), blocks_per_tile), :]
        bias = []
        for jb in range(blocks_per_tile):
            kpos = k0 + jb * SEL_BLOCK + lax.broadcasted_iota(jnp.int32, (SEL_BLOCK, 1), 0)
            bias.append(jnp.where((sel[jb:jb + 1, :] > 0.5) & (kpos <= tpos_l), 0.0, NEG_INF))
        bias = jnp.concatenate(bias, axis=0)
        s4 = _dot_nt(ks_ref[pl.ds(k0, tk), :], q4)
        for h in range(hg):
            cols = slice(h * tq, (h + 1) * tq)
            sh = (heads(s4, h) + bias) * LOG2_E
            s_scr[pl.ds(k0, tk), cols] = sh
            m_scr[:, cols] = jnp.maximum(m_scr[:, cols], sublane_tile_max(sh))
        return carry

    lax.fori_loop(0, n_tiles, scores, 0)
    m_scr[...] = jnp.broadcast_to(jnp.max(m_scr[...], axis=0, keepdims=True), m_scr.shape)
    acc_scr[...] = jnp.zeros_like(acc_scr)

    def weighted(kt, carry):
        k0 = pl.multiple_of(kt * tk, tk)
        s = s_scr[pl.ds(k0, tk), :].reshape(tk // SUBLANES, SUBLANES, hg * tq)
        e = jnp.exp2(s - m_scr[...]).reshape(tk, hg * tq).astype(BF16)
        acc_scr[...] += _dot(with_ones(vst_ref[:, pl.ds(k0, tk)]), e)
        return carry

    lax.fori_loop(0, n_tiles, weighted, 0)

    o_slc = normalised(acc_scr[...])
    gates_t = gate_ref[...].T
    for h in range(hg):
        c0 = h * N_BRANCHES
        o = (gates_t[c0:c0 + 1] * heads(oc_scr, h) + gates_t[c0 + 1:c0 + 2] * heads(o_slc, h)
             + gates_t[c0 + 2:c0 + 3] * heads(ow_scr, h))
        o_ref[:, h * HEAD_DIM:(h + 1) * HEAD_DIM] = o.T.astype(BF16)


def _attention(q, gates, kv_cmp, kv_cmp_t, kv_rest, v_rest_t, hg):
    bsz, s, _ = q.shape
    groups = kv_rest.shape[2]
    n_rows = kv_cmp.shape[3]
    n_cmp = n_rows - 1
    n_slc = s // SEL_BLOCK
    n_sel = min(N_SELECT, n_slc)
    tq = ATTN_TILE
    assert n_rows == LANES and n_slc <= LANES and n_slc % (SLC_TILE // SEL_BLOCK) == 0
    assert s % SLC_TILE == 0 and s >= WINDOW + tq

    ci = np.arange(LANES)[None, :]
    sj = np.arange(LANES)[:, None]
    overlap_t = ((ci * CMP_STRIDE <= sj * SEL_BLOCK + SEL_BLOCK - 1)
                 & (ci * CMP_STRIDE + CMP_BLOCK - 1 >= sj * SEL_BLOCK) & (ci < n_cmp) & (sj < n_slc))
    overlap_t = jnp.asarray(np.tile(overlap_t, (1, hg)), BF16)

    def k_spec(which):
        return pl.BlockSpec((None, None, None, s, HEAD_DIM), lambda bi, gi, i: (bi, which, gi, 0, 0))

    def vt_spec(which):
        return pl.BlockSpec((None, None, None, HEAD_DIM, s), lambda bi, gi, i: (bi, which, gi, 0, 0))

    def cmp_spec(which):
        return pl.BlockSpec((None, None, None, n_rows, HEAD_DIM), lambda bi, gi, i: (bi, which, gi, 0, 0))

    qblk = pl.BlockSpec((None, tq, hg * HEAD_DIM), lambda bi, gi, i: (bi, i, gi))
    return pl.pallas_call(
        functools.partial(_attn_kernel, hg=hg, n_cmp=n_cmp, n_slc=n_slc, n_sel=n_sel),
        grid=(bsz, groups, s // tq),
        in_specs=[
            qblk,
            pl.BlockSpec((None, tq, LANES), lambda bi, gi, i: (bi, i, gi)),
            cmp_spec(0), cmp_spec(1),
            k_spec(0), vt_spec(0), k_spec(2), vt_spec(1),
            pl.BlockSpec((LANES, hg * LANES), lambda bi, gi, i: (0, 0)),
        ],
        out_specs=qblk,
        out_shape=jax.ShapeDtypeStruct(q.shape, BF16),
        scratch_shapes=[pltpu.VMEM((hg * tq, HEAD_DIM), BF16),
                        pltpu.VMEM((n_slc, tq), F32),
                        pltpu.VMEM((s, hg * tq), F32),
                        pltpu.VMEM((SUBLANES, hg * tq), F32),
                        pltpu.VMEM((2 * HEAD_DIM, hg * tq), F32),
                        pltpu.VMEM((HEAD_DIM, hg * tq), F32),
                        pltpu.VMEM((HEAD_DIM, hg * tq), F32)],
        compiler_params=_params(("parallel", "parallel", "arbitrary")),
        name="nsa_attention",
    )(q, gates, kv_cmp, kv_cmp_t, kv_rest, v_rest_t, kv_rest, v_rest_t, overlap_t)


def _rope_tables(s):
    freqs = ROPE_THETA ** (-jnp.arange(HALF_HEAD, dtype=F32) / HALF_HEAD)
    ang = jnp.arange(s).astype(F32)[:, None] * freqs[None, :]
    cos, sin = jnp.cos(ang), jnp.sin(ang)
    return jnp.concatenate([cos, cos], axis=1), jnp.concatenate([-sin, sin], axis=1)


def kernel(x, c, mod_w, mod_b, norm_g, mlp_w1, mlp_w2, a_w_in, a_b_in, a_ln_g, a_ln_b, a_w_s, a_b_s, a_w_out, a_b_out, kv_norm_g, kv_mod_w, kv_mod_b, w_kv, cmp_pos_k, cmp_w1_k, cmp_w2_k, cmp_pos_v, cmp_w1_v, cmp_w2_v, b_w_qg, b_w_o, final_g):
    bsz, s, d = x.shape
    depth = mod_w.shape[0]
    n_a = a_w_in.shape[0]
    groups = w_kv.shape[1] // (2 * N_BRANCHES * HEAD_DIM)
    heads = b_w_o.shape[1] // HEAD_DIM
    hg = heads // groups
    assert hg * N_BRANCHES <= LANES

    mod = _cond_matmul(c, mod_w.reshape(depth * 2, d, 3 * d), mod_b.reshape(depth * 2, 3 * d))
    mod = mod.reshape(depth, 2, bsz, 3, 1, d)
    kv_mod = _cond_matmul(c, kv_mod_w[None], kv_mod_b[None]).reshape(bsz, 2, 1, d)
    cos, sin = _rope_tables(s)
    w1_all, w2_all = mlp_w1.astype(BF16), mlp_w2.astype(BF16)

    for layer in range(depth):
        shift, scale, gate = (mod[layer, 0, :, t] for t in range(3))
        if layer < n_a:
            x = _gmlp(x, norm_g[layer, 0], shift, scale, gate,
                      a_w_in[layer].astype(BF16), a_b_in[layer], a_ln_g[layer], a_ln_b[layer],
                      a_w_s[layer], a_b_s[layer], a_w_out[layer].astype(BF16), a_b_out[layer])
        else:
            j = layer - n_a
            if layer == n_a:
                kv_cmp, kv_rest = _kv_proj(x, kv_norm_g, kv_mod[:, 0], kv_mod[:, 1],
                                           w_kv.astype(BF16), cos, sin, groups)
                pos = jnp.stack([cmp_pos_k, cmp_pos_v])
                kv_cmp, kv_cmp_t = _compress(kv_cmp, pos,
                                             jnp.stack([cmp_w1_k, cmp_w1_v]).astype(BF16),
                                             jnp.stack([cmp_w2_k, cmp_w2_v]).astype(BF16))
                v_rest_t = jnp.swapaxes(kv_rest[:, 1::2], 3, 4)
            w_qg = b_w_qg[j]
            w_q = w_qg[:, :heads * HEAD_DIM]
            w_gate = w_qg[:, heads * HEAD_DIM:].reshape(d, groups, hg * N_BRANCHES)
            w_gate = jnp.pad(w_gate, ((0, 0), (0, 0), (0, LANES - hg * N_BRANCHES)))
            w_gate = w_gate.reshape(d, groups * LANES)
            tn = hg * HEAD_DIM
            pad = (-w_gate.shape[1]) % tn
            w_gate = jnp.pad(w_gate, ((0, 0), (0, pad)))
            q, gates = _qg_proj(x, norm_g[layer, 0], shift, scale,
                                w_q.astype(BF16), w_gate.astype(BF16), cos, sin, tn)
            o = _attention(q, gates, kv_cmp, kv_cmp_t, kv_rest, v_rest_t, hg)
            x = _mm_residual(o, b_w_o[j].astype(BF16), x, gate)

        shift, scale, gate = (mod[layer, 1, :, t] for t in range(3))
        x = _mlp(x, norm_g[layer, 1], shift, scale, gate, w1_all, w2_all, layer,
                 final_g, final_norm=(layer == depth - 1))
    return x
```

```python
import functools

import numpy as np
import jax
import jax.numpy as jnp
from jax import lax
from jax.experimental import pallas as pl
from jax.experimental.pallas import tpu as pltpu

BF16 = jnp.bfloat16
F32 = jnp.float32

LANES = 128
SUBLANES = 8
MXU_COLS = 256
HEAD_DIM = 128
HALF_HEAD = HEAD_DIM // 2
N_BRANCHES = 3
CMP_BLOCK = 32
CMP_STRIDE = 16
SEL_BLOCK = 64
N_SELECT = 8
WINDOW = 512
ROPE_THETA = 10000.0
EPS = 1e-6
NEG_INF = -1e30
FORCE_BONUS = 1e6
LOG2_E = 1.4426950408889634
ATTN_TILE = 512
SLC_TILE = 512
VMEM_LIMIT = 56 * 1024 * 1024


def _dot(a, b):
    return jnp.dot(a, b, preferred_element_type=F32)


def _dot_nt(a, b):
    return lax.dot_general(a, b, (((1,), (1,)), ((), ())), preferred_element_type=F32)


def _tile(n, pref):
    return pref if n % pref == 0 else n


def _params(semantics):
    return pltpu.CompilerParams(dimension_semantics=semantics, vmem_limit_bytes=VMEM_LIMIT)


def _rms(x, g):
    return (x * lax.rsqrt(jnp.mean(x * x, axis=-1, keepdims=True) + EPS)) * g


def _norm_mod(x, g, shift, scale):
    return _rms(x, g) * (1.0 + scale) + shift


def _rope(a, cos, sin):
    return a * cos + pltpu.roll(a, HALF_HEAD, 1) * sin


def _dot_chunks(h_ref, w_ref, epilogue, chunk=MXU_COLS):
    n = w_ref.shape[1]
    chunk = chunk if n % chunk == 0 else n
    for c0 in range(0, n, chunk):
        cols = slice(c0, c0 + chunk)
        epilogue(cols, _dot(h_ref[...], w_ref[:, cols]))


def _cond_kernel(c_ref, w_ref, b_ref, o_ref):
    cond = jax.nn.silu(c_ref[...]).astype(BF16)
    o_ref[...] = _dot(cond, w_ref[...].astype(BF16)) + b_ref[...]


def _cond_matmul(c, w, b):
    n, d, e = w.shape
    bsz = c.shape[0]
    te = _tile(e, 1024)
    return pl.pallas_call(
        _cond_kernel,
        grid=(n, e // te),
        in_specs=[
            pl.BlockSpec((bsz, d), lambda s, j: (0, 0)),
            pl.BlockSpec((None, d, te), lambda s, j: (s, 0, j)),
            pl.BlockSpec((None, 1, te), lambda s, j: (s, 0, j)),
        ],
        out_specs=pl.BlockSpec((None, bsz, te), lambda s, j: (s, 0, j)),
        out_shape=jax.ShapeDtypeStruct((n, bsz, e), F32),
        compiler_params=_params(("parallel", "parallel")),
        name="cond_matmul",
    )(c, w, b.reshape(n, 1, e))


def _gmlp_kernel(x_ref, g_ref, sh_ref, sc_ref, gate_ref, win_ref, bin_ref, lg_ref, lb_ref, ws_ref,
                 bs_ref, wout_ref, bout_ref, o_ref, h_scr, v_scr, mix_scr, mu_scr, rs_scr,
                 *, n_half, gdim, chunk):
    j = pl.program_id(2)
    tm = v_scr.shape[0]
    tn = win_ref.shape[1]
    tn_out = wout_ref.shape[1]

    @pl.when(j == 0)
    def _():
        h_scr[...] = _norm_mod(x_ref[...], g_ref[...], sh_ref[...], sc_ref[...]).astype(BF16)

    @pl.when(j < n_half)
    def _():
        base = pl.multiple_of(j * tn, tn)

        def store(cols, acc):
            v_scr[:, pl.ds(base + cols.start, cols.stop - cols.start)] = jax.nn.gelu(acc + bin_ref[:, cols])
        _dot_chunks(h_scr, win_ref, store, chunk=2 * MXU_COLS)

    @pl.when(j == n_half - 1)
    def _():
        v = v_scr[...]
        mu = jnp.mean(v, axis=-1, keepdims=True)
        dv = v - mu
        mu_scr[...] = mu
        rs_scr[...] = lax.rsqrt(jnp.mean(dv * dv, axis=-1, keepdims=True) + EPS)

    @pl.when((j >= n_half) & (j < 2 * n_half))
    def _():
        jb = j - n_half
        base = pl.multiple_of(jb * tn, tn)
        causal = (lax.broadcasted_iota(jnp.int32, (chunk, chunk), 1)
                  <= lax.broadcasted_iota(jnp.int32, (chunk, chunk), 0))

        def store(cols, acc):
            u = jax.nn.gelu(acc + bin_ref[:, cols])
            vcols = pl.ds(base + cols.start, gdim)
            vn = (((v_scr[:, vcols] - mu_scr[...]) * rs_scr[...]) * lg_ref[:, cols]
                  + lb_ref[:, cols]).astype(BF16)
            gi = jb * (tn // gdim) + cols.start // gdim
            w = jnp.where(causal, ws_ref[gi], 0.0).astype(BF16)
            for r0 in range(0, tm, chunk):
                rows = slice(r0, r0 + chunk)
                mixed = _dot(w, vn[rows]) + bs_ref[gi]
                mix_scr[rows, vcols] = (u[rows] * mixed).astype(BF16)
        _dot_chunks(h_scr, win_ref, store, chunk=gdim)

    @pl.when(j >= 2 * n_half)
    def _():
        base = pl.multiple_of((j - 2 * n_half) * tn_out, tn_out)

        def store(cols, acc):
            xcols = pl.ds(base + cols.start, cols.stop - cols.start)
            o_ref[:, cols] = x_ref[:, xcols] + gate_ref[:, cols] * (acc + bout_ref[:, cols])
        _dot_chunks(mix_scr, wout_ref, store)


def _gmlp(x, g, shift, scale, gate, w_in, b_in, ln_g, ln_b, w_s, b_s, w_out, b_out):
    bsz, s, d = x.shape
    width = w_in.shape[1] // 2
    groups, chunk, _ = w_s.shape
    gdim = width // groups
    tm = _tile(s, 512)
    tn = _tile(width, 1024)
    tn_out = _tile(d, 512)
    n_half = width // tn
    n_out = d // tn_out
    assert tn % gdim == 0 and gdim % LANES == 0 and tm % chunk == 0

    def in_col(j):
        return jnp.where(j < n_half, j + n_half, jnp.minimum(j - n_half, n_half - 1))

    def u_col(j):
        return jnp.clip(j - n_half, 0, n_half - 1)

    def out_col(j):
        return jnp.clip(j - 2 * n_half, 0, n_out - 1)

    row = lambda bi, i, j: (bi, 0, 0)
    return pl.pallas_call(
        functools.partial(_gmlp_kernel, n_half=n_half, gdim=gdim, chunk=chunk),
        grid=(bsz, s // tm, 2 * n_half + n_out),
        in_specs=[
            pl.BlockSpec((None, tm, d), lambda bi, i, j: (bi, i, 0)),
            pl.BlockSpec((1, d), lambda bi, i, j: (0, 0)),
            pl.BlockSpec((None, 1, d), row),
            pl.BlockSpec((None, 1, d), row),
            pl.BlockSpec((None, 1, tn_out), lambda bi, i, j: (bi, 0, out_col(j))),
            pl.BlockSpec((d, tn), lambda bi, i, j: (0, in_col(j))),
            pl.BlockSpec((1, tn), lambda bi, i, j: (0, in_col(j))),
            pl.BlockSpec((1, tn), lambda bi, i, j: (0, u_col(j))),
            pl.BlockSpec((1, tn), lambda bi, i, j: (0, u_col(j))),
            pl.BlockSpec((groups, chunk, chunk), lambda bi, i, j: (0, 0, 0)),
            pl.BlockSpec((groups, chunk, 1), lambda bi, i, j: (0, 0, 0)),
            pl.BlockSpec((width, tn_out), lambda bi, i, j: (0, out_col(j))),
            pl.BlockSpec((1, tn_out), lambda bi, i, j: (0, out_col(j))),
        ],
        out_specs=pl.BlockSpec((None, tm, tn_out), lambda bi, i, j: (bi, i, out_col(j))),
        out_shape=jax.ShapeDtypeStruct((bsz, s, d), F32),
        scratch_shapes=[pltpu.VMEM((tm, d), BF16),
                        pltpu.VMEM((tm, width), F32),
                        pltpu.VMEM((tm, width), BF16),
                        pltpu.VMEM((tm, 1), F32), pltpu.VMEM((tm, 1), F32)],
        compiler_params=_params(("parallel", "parallel", "arbitrary")),
        name="gmlp",
    )(x, g.reshape(1, d), shift, scale, gate, w_in, b_in.reshape(1, 2 * width),
      ln_g.reshape(1, width), ln_b.reshape(1, width), w_s, b_s.reshape(groups, chunk, 1),
      w_out, b_out.reshape(1, d))


def _mm_res_kernel(a_ref, w_ref, x_ref, gate_ref, o_ref):
    def store(cols, acc):
        o_ref[:, cols] = x_ref[:, cols] + gate_ref[:, cols] * acc
    _dot_chunks(a_ref, w_ref, store)


def _mm_residual(a, w, x, gate):
    bsz, s, k = a.shape
    d = w.shape[1]
    tm = _tile(s, 512)
    tn = _tile(d, 2048)
    return pl.pallas_call(
        _mm_res_kernel,
        grid=(bsz, s // tm, d // tn),
        in_specs=[
            pl.BlockSpec((None, tm, k), lambda bi, i, j: (bi, i, 0)),
            pl.BlockSpec((k, tn), lambda bi, i, j: (0, j)),
            pl.BlockSpec((None, tm, tn), lambda bi, i, j: (bi, i, j)),
            pl.BlockSpec((None, 1, tn), lambda bi, i, j: (bi, 0, j)),
        ],
        out_specs=pl.BlockSpec((None, tm, tn), lambda bi, i, j: (bi, i, j)),
        out_shape=jax.ShapeDtypeStruct((bsz, s, d), F32),
        compiler_params=_params(("parallel", "parallel", "parallel")),
        name="matmul_residual",
    )(a, w, x, gate)


def _mlp_kernel(x_ref, g_ref, sh_ref, sc_ref, gate_ref, w1_ref, w2_ref, fg_ref, o_ref,
                h_scr, acc_scr, *, final_norm):
    f = pl.program_id(2)

    @pl.when(f == 0)
    def _():
        h_scr[...] = _norm_mod(x_ref[...], g_ref[...], sh_ref[...], sc_ref[...]).astype(BF16)
        acc_scr[...] = jnp.zeros_like(acc_scr)

    a = jnp.square(jnp.maximum(_dot(h_scr[...], w1_ref[...]), 0.0)).astype(BF16)
    acc_scr[...] += _dot(a, w2_ref[...])

    @pl.when(f == pl.num_programs(2) - 1)
    def _():
        xn = x_ref[...] + gate_ref[...] * acc_scr[...]
        if final_norm:
            xn = _rms(xn, fg_ref[...])
        o_ref[...] = xn


def _mlp(x, g, shift, scale, gate, w1, w2, layer, final_g, final_norm):
    bsz, s, d = x.shape
    dff = w1.shape[2]
    tm = _tile(s, 512)
    tf = _tile(dff, 1024)
    row = lambda bi, i, f: (bi, 0, 0)
    vec = pl.BlockSpec((1, d), lambda bi, i, f: (0, 0))
    xblk = pl.BlockSpec((None, tm, d), lambda bi, i, f: (bi, i, 0))
    return pl.pallas_call(
        functools.partial(_mlp_kernel, final_norm=final_norm),
        grid=(bsz, s // tm, dff // tf),
        in_specs=[
            xblk, vec,
            pl.BlockSpec((None, 1, d), row), pl.BlockSpec((None, 1, d), row),
            pl.BlockSpec((None, 1, d), row),
            pl.BlockSpec((None, d, tf), lambda bi, i, f: (layer, 0, f)),
            pl.BlockSpec((None, tf, d), lambda bi, i, f: (layer, f, 0)),
            vec,
        ],
        out_specs=xblk,
        out_shape=jax.ShapeDtypeStruct((bsz, s, d), F32),
        scratch_shapes=[pltpu.VMEM((tm, d), BF16), pltpu.VMEM((tm, d), F32)],
        compiler_params=_params(("parallel", "parallel", "arbitrary")),
        name="mlp",
    )(x, g.reshape(1, d), shift, scale, gate, w1, w2, final_g.reshape(1, d))


def _kv_kernel(x_ref, g_ref, sh_ref, sc_ref, w_ref, cos_ref, sin_ref, cmp_ref, rest_ref, h_scr,
               *, groups):
    j = pl.program_id(2)

    @pl.when(j == 0)
    def _():
        h_scr[...] = _norm_mod(x_ref[...], g_ref[...], sh_ref[...], sc_ref[...]).astype(BF16)

    def emit(dst, rotate):
        def store(cols, acc):
            for c0 in range(0, acc.shape[1], HEAD_DIM):
                a = acc[:, c0:c0 + HEAD_DIM]
                if rotate:
                    a = _rope(a, cos_ref[...], sin_ref[...])
                dst[(cols.start + c0) // HEAD_DIM] = a.astype(dst.dtype)
        _dot_chunks(h_scr, w_ref, store)

    @pl.when(j == 0)
    def _():
        emit(cmp_ref, True)

    @pl.when(j == 1)
    def _():
        emit(cmp_ref, False)

    @pl.when((j >= 2) & (j % 2 == 0))
    def _():
        emit(rest_ref, True)

    @pl.when((j >= 2) & (j % 2 == 1))
    def _():
        emit(rest_ref, False)


def _kv_proj(x, g, shift, scale, w, cos, sin, groups):
    bsz, s, d = x.shape
    tm = _tile(s, 1024)
    tn = groups * HEAD_DIM
    row = lambda bi, i, j: (bi, 0, 0)
    tab = pl.BlockSpec((tm, HEAD_DIM), lambda bi, i, j: (i, 0))
    return pl.pallas_call(
        functools.partial(_kv_kernel, groups=groups),
        grid=(bsz, s // tm, 2 * N_BRANCHES),
        in_specs=[
            pl.BlockSpec((None, tm, d), lambda bi, i, j: (bi, i, 0)),
            pl.BlockSpec((1, d), lambda bi, i, j: (0, 0)),
            pl.BlockSpec((None, 1, d), row), pl.BlockSpec((None, 1, d), row),
            pl.BlockSpec((d, tn), lambda bi, i, j: (0, j)),
            tab, tab,
        ],
        out_specs=[
            pl.BlockSpec((None, None, groups, tm, HEAD_DIM),
                         lambda bi, i, j: (bi, jnp.minimum(j, 1), 0, i, 0)),
            pl.BlockSpec((None, None, groups, tm, HEAD_DIM),
                         lambda bi, i, j: (bi, jnp.maximum(j - 2, 0), 0, i, 0)),
        ],
        out_shape=[jax.ShapeDtypeStruct((bsz, 2, groups, s, HEAD_DIM), F32),
                   jax.ShapeDtypeStruct((bsz, 4, groups, s, HEAD_DIM), BF16)],
        scratch_shapes=[pltpu.VMEM((tm, d), BF16)],
        compiler_params=_params(("parallel", "parallel", "arbitrary")),
        name="kv_proj",
    )(x, g.reshape(1, d), shift, scale, w, cos, sin)


def _qg_kernel(x_ref, g_ref, sh_ref, sc_ref, w_ref, cos_ref, sin_ref, q_ref, gate_ref, h_scr,
               *, heads_per_tile, n_q):
    j = pl.program_id(2)

    @pl.when(j == 0)
    def _():
        h_scr[...] = _norm_mod(x_ref[...], g_ref[...], sh_ref[...], sc_ref[...]).astype(BF16)

    @pl.when(j < n_q)
    def _():
        def store(cols, acc):
            for c0 in range(0, acc.shape[1], HEAD_DIM):
                a = _rope(acc[:, c0:c0 + HEAD_DIM], cos_ref[...], sin_ref[...]) * (HEAD_DIM ** -0.5)
                q_ref[:, cols.start + c0:cols.start + c0 + HEAD_DIM] = a.astype(BF16)
        _dot_chunks(h_scr, w_ref, store)

    @pl.when(j >= n_q)
    def _():
        def store(cols, acc):
            gate_ref[:, cols] = jax.nn.sigmoid(acc)
        _dot_chunks(h_scr, w_ref, store)


def _qg_proj(x, g, shift, scale, w_q, w_gate, cos, sin, tn):
    bsz, s, d = x.shape
    nq_cols, ng_cols = w_q.shape[1], w_gate.shape[1]
    n_q, n_g = nq_cols // tn, ng_cols // tn
    tm = _tile(s, 1024)
    row = lambda bi, i, j: (bi, 0, 0)
    tab = pl.BlockSpec((tm, HEAD_DIM), lambda bi, i, j: (i, 0))
    w = jnp.concatenate([w_q, w_gate], axis=1)
    return pl.pallas_call(
        functools.partial(_qg_kernel, heads_per_tile=tn // HEAD_DIM, n_q=n_q),
        grid=(bsz, s // tm, n_q + n_g),
        in_specs=[
            pl.BlockSpec((None, tm, d), lambda bi, i, j: (bi, i, 0)),
            pl.BlockSpec((1, d), lambda bi, i, j: (0, 0)),
            pl.BlockSpec((None, 1, d), row), pl.BlockSpec((None, 1, d), row),
            pl.BlockSpec((d, tn), lambda bi, i, j: (0, j)),
            tab, tab,
        ],
        out_specs=[
            pl.BlockSpec((None, tm, tn), lambda bi, i, j: (bi, i, jnp.minimum(j, n_q - 1))),
            pl.BlockSpec((None, tm, tn), lambda bi, i, j: (bi, i, jnp.maximum(j - n_q, 0))),
        ],
        out_shape=[jax.ShapeDtypeStruct((bsz, s, nq_cols), BF16),
                   jax.ShapeDtypeStruct((bsz, s, ng_cols), F32)],
        scratch_shapes=[pltpu.VMEM((tm, d), BF16)],
        compiler_params=_params(("parallel", "parallel", "arbitrary")),
        name="qg_proj",
    )(x, g.reshape(1, d), shift, scale, w, cos, sin)


def _compress_kernel(kv_ref, pos_ref, w1_ref, w2_ref, w2t_ref, o_ref, ot_ref):
    n_rows = kv_ref.shape[0] // CMP_STRIDE

    def half_block(p0):
        total = None
        for p in range(0, CMP_STRIDE, 2):
            a = jnp.concatenate(
                [kv_ref[pl.ds(p + q, n_rows, stride=CMP_STRIDE), :] + pos_ref[p0 + p + q:p0 + p + q + 1, :]
                 for q in range(2)], axis=1).astype(BF16)
            part = _dot(a, w1_ref[(p0 + p) * HEAD_DIM:(p0 + p + 2) * HEAD_DIM, :])
            total = part if total is None else total + part
        return total

    first = half_block(0)
    second = half_block(CMP_STRIDE)
    pre = first + pltpu.roll(second, n_rows - 1, 0)
    hidden = jax.nn.gelu(pre).astype(BF16)
    o_ref[...] = _dot(hidden, w2_ref[...]).astype(BF16)
    ot_ref[...] = _dot_nt(w2t_ref[...], hidden).astype(BF16)


def _compress(kv_cmp, pos, w1, w2):
    bsz, _, groups, s, _ = kv_cmp.shape
    n_rows = s // CMP_STRIDE
    hidden = w1.shape[2]
    return pl.pallas_call(
        _compress_kernel,
        grid=(bsz, 2, groups),
        in_specs=[
            pl.BlockSpec((None, None, None, s, HEAD_DIM), lambda bi, t, gi: (bi, t, gi, 0, 0)),
            pl.BlockSpec((None, CMP_BLOCK, HEAD_DIM), lambda bi, t, gi: (t, 0, 0)),
            pl.BlockSpec((None, CMP_BLOCK * HEAD_DIM, hidden), lambda bi, t, gi: (t, 0, 0)),
            pl.BlockSpec((None, hidden, HEAD_DIM), lambda bi, t, gi: (t, 0, 0)),
            pl.BlockSpec((None, HEAD_DIM, hidden), lambda bi, t, gi: (t, 0, 0)),
        ],
        out_specs=[pl.BlockSpec((None, None, None, n_rows, HEAD_DIM), lambda bi, t, gi: (bi, t, gi, 0, 0)),
                   pl.BlockSpec((None, None, None, HEAD_DIM, n_rows), lambda bi, t, gi: (bi, t, gi, 0, 0))],
        out_shape=[jax.ShapeDtypeStruct((bsz, 2, groups, n_rows, HEAD_DIM), BF16),
                   jax.ShapeDtypeStruct((bsz, 2, groups, HEAD_DIM, n_rows), BF16)],
        compiler_params=_params(("parallel", "parallel", "parallel")),
        name="compress",
    )(kv_cmp, pos, w1, w2, jnp.swapaxes(w2, 1, 2))


def _masked_softmax(s, mask, axis):
    s = jnp.where(mask, s, NEG_INF)
    m = jnp.max(s, axis=axis, keepdims=True)
    e = jnp.where(mask, jnp.exp(s - m), 0.0)
    den = jnp.sum(e, axis=axis, keepdims=True)
    return e / jnp.maximum(den, 1e-30)


def _attn_kernel(q_ref, gate_ref, kc_ref, vct_ref, ks_ref, vst_ref, kw_ref, vwt_ref, ovt_ref,
                 o_ref, q4_scr, sel_scr, s_scr, m_scr, acc_scr, oc_scr, ow_scr,
                 *, hg, n_cmp, n_slc, n_sel):
    tq = ATTN_TILE
    tk = SLC_TILE
    i = pl.program_id(2)
    t0 = i * tq
    for h in range(hg):
        q4_scr[h * tq:(h + 1) * tq, :] = q_ref[:, h * HEAD_DIM:(h + 1) * HEAD_DIM]
    q4 = q4_scr[...]
    tpos_l = t0 + lax.broadcasted_iota(jnp.int32, (1, tq), 1)
    tpos4_l = jnp.concatenate([tpos_l] * hg, axis=1)

    def heads(a, h):
        return a[:, h * tq:(h + 1) * tq]

    n_idx = lax.broadcasted_iota(jnp.int32, (kc_ref.shape[0], 1), 0)
    valid = (n_idx * CMP_STRIDE + (CMP_BLOCK - 1) <= tpos4_l) & (n_idx < n_cmp)
    p_cmp = _masked_softmax(_dot_nt(kc_ref[...], q4), valid, 0).astype(BF16)
    oc_scr[...] = _dot(vct_ref[...], p_cmp)

    p_heads = jnp.concatenate([heads(p_cmp, h) for h in range(hg)], axis=0)
    imp = _dot(ovt_ref[...], p_heads)[:n_slc]
    blk = lax.broadcasted_iota(jnp.int32, (n_slc, 1), 0)
    cur = tpos_l // SEL_BLOCK
    forced = (blk == 0) | (blk == cur) | (blk == cur - 1)
    imp = jnp.where(blk * SEL_BLOCK <= tpos_l, imp + jnp.where(forced, FORCE_BONUS, 0.0), NEG_INF)
    rank = jnp.zeros((n_slc, tq), jnp.int32)
    for c in range(n_slc):
        row = imp[c:c + 1, :]
        beats = (row > imp) | ((row == imp) & (c < blk))
        rank = rank + jnp.where(beats, 1, 0)
    sel_scr[...] = jnp.where(rank < n_sel, 1.0, 0.0)

    def with_ones(vt):
        return jnp.concatenate([vt, jnp.ones(vt.shape, BF16)], axis=0)

    def normalised(acc):
        return acc[:HEAD_DIM] / jnp.maximum(acc[HEAD_DIM:], 1e-30)

    def sublane_tile_max(sh):
        return jnp.max(sh.reshape(sh.shape[0] // SUBLANES, SUBLANES, sh.shape[1]), axis=0)

    wk = WINDOW + tq
    kw0 = pl.multiple_of(jnp.maximum(t0 - WINDOW, 0), tq)
    kpos = kw0 + lax.broadcasted_iota(jnp.int32, (wk, 1), 0)
    bias = jnp.where((kpos <= tpos_l) & (kpos > tpos_l - WINDOW), 0.0, NEG_INF)
    s4 = _dot_nt(kw_ref[pl.ds(kw0, wk), :], q4)
    e = []
    for h in range(hg):
        sh = heads(s4, h) + bias
        m = jnp.max(sublane_tile_max(sh), axis=0, keepdims=True)
        e.append(jnp.exp(sh - m).astype(BF16))
    ow_scr[...] = normalised(_dot(with_ones(vwt_ref[:, pl.ds(kw0, wk)]), jnp.concatenate(e, axis=1)))

    n_tiles = i // (tk // tq) + 1
    blocks_per_tile = tk // SEL_BLOCK
    m_scr[...] = jnp.full_like(m_scr, NEG_INF)

    def scores(kt, carry):
        k0 = pl.multiple_of(kt * tk, tk)
        sel = sel_scr[pl.ds(pl.multiple_of(kt * blocks_per_tile, blocks_per_tile), blocks_per_tile), :]
        bias = []
        for jb in range(blocks_per_tile):
            kpos = k0 + jb * SEL_BLOCK + lax.broadcasted_iota(jnp.int32, (SEL_BLOCK, 1), 0)
            bias.append(jnp.where((sel[jb:jb + 1, :] > 0.5) & (kpos <= tpos_l), 0.0, NEG_INF))
        bias = jnp.concatenate(bias, axis=0)
        s4 = _dot_nt(ks_ref[pl.ds(k0, tk), :], q4)
        for h in range(hg):
            cols = slice(h * tq, (h + 1) * tq)
            sh = (heads(s4, h) + bias) * LOG2_E
            s_scr[pl.ds(k0, tk), cols] = sh
            m_scr[:, cols] = jnp.maximum(m_scr[:, cols], sublane_tile_max(sh))
        return carry

    lax.fori_loop(0, n_tiles, scores, 0)
    m_scr[...] = jnp.broadcast_to(jnp.max(m_scr[...], axis=0, keepdims=True), m_scr.shape)
    acc_scr[...] = jnp.zeros_like(acc_scr)

    def weighted(kt, carry):
        k0 = pl.multiple_of(kt * tk, tk)
        s = s_scr[pl.ds(k0, tk), :].reshape(tk // SUBLANES, SUBLANES, hg * tq)
        e = jnp.exp2(s - m_scr[...]).reshape(tk, hg * tq).astype(BF16)
        acc_scr[...] += _dot(with_ones(vst_ref[:, pl.ds(k0, tk)]), e)
        return carry

    lax.fori_loop(0, n_tiles, weighted, 0)

    o_slc = normalised(acc_scr[...])
    gates_t = gate_ref[...].T
    for h in range(hg):
        c0 = h * N_BRANCHES
        o = (gates_t[c0:c0 + 1] * heads(oc_scr, h) + gates_t[c0 + 1:c0 + 2] * heads(o_slc, h)
             + gates_t[c0 + 2:c0 + 3] * heads(ow_scr, h))
        o_ref[:, h * HEAD_DIM:(h + 1) * HEAD_DIM] = o.T.astype(BF16)


def _attention(q, gates, kv_cmp, kv_cmp_t, kv_rest, v_rest_t, hg):
    bsz, s, _ = q.shape
    groups = kv_rest.shape[2]
    n_rows = kv_cmp.shape[3]
    n_cmp = n_rows - 1
    n_slc = s // SEL_BLOCK
    n_sel = min(N_SELECT, n_slc)
    tq = ATTN_TILE
    assert n_rows == LANES and n_slc <= LANES and n_slc % (SLC_TILE // SEL_BLOCK) == 0
    assert s % SLC_TILE == 0 and s >= WINDOW + tq

    ci = np.arange(LANES)[None, :]
    sj = np.arange(LANES)[:, None]
    overlap_t = ((ci * CMP_STRIDE <= sj * SEL_BLOCK + SEL_BLOCK - 1)
                 & (ci * CMP_STRIDE + CMP_BLOCK - 1 >= sj * SEL_BLOCK) & (ci < n_cmp) & (sj < n_slc))
    overlap_t = jnp.asarray(np.tile(overlap_t, (1, hg)), BF16)

    def k_spec(which):
        return pl.BlockSpec((None, None, None, s, HEAD_DIM), lambda bi, gi, i: (bi, which, gi, 0, 0))

    def vt_spec(which):
        return pl.BlockSpec((None, None, None, HEAD_DIM, s), lambda bi, gi, i: (bi, which, gi, 0, 0))

    def cmp_spec(which):
        return pl.BlockSpec((None, None, None, n_rows, HEAD_DIM), lambda bi, gi, i: (bi, which, gi, 0, 0))

    qblk = pl.BlockSpec((None, tq, hg * HEAD_DIM), lambda bi, gi, i: (bi, i, gi))
    return pl.pallas_call(
        functools.partial(_attn_kernel, hg=hg, n_cmp=n_cmp, n_slc=n_slc, n_sel=n_sel),
        grid=(bsz, groups, s // tq),
        in_specs=[
            qblk,
            pl.BlockSpec((None, tq, LANES), lambda bi, gi, i: (bi, i, gi)),
            cmp_spec(0), cmp_spec(1),
            k_spec(0), vt_spec(0), k_spec(2), vt_spec(1),
            pl.BlockSpec((LANES, hg * LANES), lambda bi, gi, i: (0, 0)),
        ],
        out_specs=qblk,
        out_shape=jax.ShapeDtypeStruct(q.shape, BF16),
        scratch_shapes=[pltpu.VMEM((hg * tq, HEAD_DIM), BF16),
                        pltpu.VMEM((n_slc, tq), F32),
                        pltpu.VMEM((s, hg * tq), F32),
                        pltpu.VMEM((SUBLANES, hg * tq), F32),
                        pltpu.VMEM((2 * HEAD_DIM, hg * tq), F32),
                        pltpu.VMEM((HEAD_DIM, hg * tq), F32),
                        pltpu.VMEM((HEAD_DIM, hg * tq), F32)],
        compiler_params=_params(("parallel", "parallel", "arbitrary")),
        name="nsa_attention",
    )(q, gates, kv_cmp, kv_cmp_t, kv_rest, v_rest_t, kv_rest, v_rest_t, overlap_t)


def _rope_tables(s):
    freqs = ROPE_THETA ** (-jnp.arange(HALF_HEAD, dtype=F32) / HALF_HEAD)
    ang = jnp.arange(s).astype(F32)[:, None] * freqs[None, :]
    cos, sin = jnp.cos(ang), jnp.sin(ang)
    return jnp.concatenate([cos, cos], axis=1), jnp.concatenate([-sin, sin], axis=1)


def kernel(x, c, mod_w, mod_b, norm_g, mlp_w1, mlp_w2, a_w_in, a_b_in, a_ln_g, a_ln_b, a_w_s, a_b_s, a_w_out, a_b_out, kv_norm_g, kv_mod_w, kv_mod_b, w_kv, cmp_pos_k, cmp_w1_k, cmp_w2_k, cmp_pos_v, cmp_w1_v, cmp_w2_v, b_w_qg, b_w_o, final_g):
    bsz, s, d = x.shape
    depth = mod_w.shape[0]
    n_a = a_w_in.shape[0]
    groups = w_kv.shape[1] // (2 * N_BRANCHES * HEAD_DIM)
    heads = b_w_o.shape[1] // HEAD_DIM
    hg = heads // groups
    assert hg * N_BRANCHES <= LANES

    mod = _cond_matmul(c, mod_w.reshape(depth * 2, d, 3 * d), mod_b.reshape(depth * 2, 3 * d))
    mod = mod.reshape(depth, 2, bsz, 3, 1, d)
    kv_mod = _cond_matmul(c, kv_mod_w[None], kv_mod_b[None]).reshape(bsz, 2, 1, d)
    cos, sin = _rope_tables(s)
    w1_all, w2_all = mlp_w1.astype(BF16), mlp_w2.astype(BF16)

    for layer in range(depth):
        shift, scale, gate = (mod[layer, 0, :, t] for t in range(3))
        if layer < n_a:
            x = _gmlp(x, norm_g[layer, 0], shift, scale, gate,
                      a_w_in[layer].astype(BF16), a_b_in[layer], a_ln_g[layer], a_ln_b[layer],
                      a_w_s[layer], a_b_s[layer], a_w_out[layer].astype(BF16), a_b_out[layer])
        else:
            j = layer - n_a
            if layer == n_a:
                kv_cmp, kv_rest = _kv_proj(x, kv_norm_g, kv_mod[:, 0], kv_mod[:, 1],
                                           w_kv.astype(BF16), cos, sin, groups)
                pos = jnp.stack([cmp_pos_k, cmp_pos_v])
                kv_cmp, kv_cmp_t = _compress(kv_cmp, pos,
                                             jnp.stack([cmp_w1_k, cmp_w1_v]).astype(BF16),
                                             jnp.stack([cmp_w2_k, cmp_w2_v]).astype(BF16))
                v_rest_t = jnp.swapaxes(kv_rest[:, 1::2], 3, 4)
            w_qg = b_w_qg[j]
            w_q = w_qg[:, :heads * HEAD_DIM]
            w_gate = w_qg[:, heads * HEAD_DIM:].reshape(d, groups, hg * N_BRANCHES)
            w_gate = jnp.pad(w_gate, ((0, 0), (0, 0), (0, LANES - hg * N_BRANCHES)))
            w_gate = w_gate.reshape(d, groups * LANES)
            tn = hg * HEAD_DIM
            pad = (-w_gate.shape[1]) % tn
            w_gate = jnp.pad(w_gate, ((0, 0), (0, pad)))
            q, gates = _qg_proj(x, norm_g[layer, 0], shift, scale,
                                w_q.astype(BF16), w_gate.astype(BF16), cos, sin, tn)
            o = _attention(q, gates, kv_cmp, kv_cmp_t, kv_rest, v_rest_t, hg)
            x = _mm_residual(o, b_w_o[j].astype(BF16), x, gate)

        shift, scale, gate = (mod[layer, 1, :, t] for t in range(3))
        x = _mlp(x, norm_g[layer, 1], shift, scale, gate, w1_all, w2_all, layer,
                 final_g, final_norm=(layer == depth - 1))
    return x
```

```python
import functools

import numpy as np
import jax
import jax.numpy as jnp
from jax import lax
from jax.experimental import pallas as pl
from jax.experimental.pallas import tpu as pltpu

BF16 = jnp.bfloat16
F32 = jnp.float32

LANES = 128
SUBLANES = 8
MXU_COLS = 256
HEAD_DIM = 128
HALF_HEAD = HEAD_DIM // 2
N_BRANCHES = 3
CMP_BLOCK = 32
CMP_STRIDE = 16
SEL_BLOCK = 64
N_SELECT = 8
WINDOW = 512
ROPE_THETA = 10000.0
EPS = 1e-6
NEG_INF = -1e30
FORCE_BONUS = 1e6
LOG2_E = 1.4426950408889634
ATTN_TILE = 512
SLC_TILE = 512
VMEM_LIMIT = 56 * 1024 * 1024


def _dot(a, b):
    return jnp.dot(a, b, preferred_element_type=F32)


def _dot_nt(a, b):
    return lax.dot_general(a, b, (((1,), (1,)), ((), ())), preferred_element_type=F32)


def _tile(n, pref):
    return pref if n % pref == 0 else n


def _col_blocks(w, tn):
    k, n = w.shape
    return w.reshape(k, n // tn, tn).transpose(1, 0, 2)


def _params(semantics):
    return pltpu.CompilerParams(dimension_semantics=semantics, vmem_limit_bytes=VMEM_LIMIT)


def _rms(x, g):
    return (x * lax.rsqrt(jnp.mean(x * x, axis=-1, keepdims=True) + EPS)) * g


def _norm_mod(x, g, shift, scale):
    return _rms(x, g) * (1.0 + scale) + shift


def _rope(a, cos, sin):
    return a * cos + pltpu.roll(a, HALF_HEAD, 1) * sin


def _dot_chunks(h_ref, w_ref, epilogue, chunk=MXU_COLS):
    n = w_ref.shape[1]
    chunk = chunk if n % chunk == 0 else n
    for c0 in range(0, n, chunk):
        cols = slice(c0, c0 + chunk)
        epilogue(cols, _dot(h_ref[...], w_ref[:, cols]))


def _cond_kernel(c_ref, w_ref, b_ref, o_ref):
    cond = jax.nn.silu(c_ref[...]).astype(BF16)
    o_ref[...] = _dot(cond, w_ref[...].astype(BF16)) + b_ref[...]


def _cond_matmul(c, w, b):
    n, d, e = w.shape
    bsz = c.shape[0]
    te = _tile(e, 1024)
    return pl.pallas_call(
        _cond_kernel,
        grid=(n, e // te),
        in_specs=[
            pl.BlockSpec((bsz, d), lambda s, j: (0, 0)),
            pl.BlockSpec((None, d, te), lambda s, j: (s, 0, j)),
            pl.BlockSpec((None, 1, te), lambda s, j: (s, 0, j)),
        ],
        out_specs=pl.BlockSpec((None, bsz, te), lambda s, j: (s, 0, j)),
        out_shape=jax.ShapeDtypeStruct((n, bsz, e), F32),
        compiler_params=_params(("parallel", "parallel")),
        name="cond_matmul",
    )(c, w, b.reshape(n, 1, e))


def _gmlp_kernel(x_ref, g_ref, sh_ref, sc_ref, gate_ref, win_ref, bin_ref, lg_ref, lb_ref, ws_ref,
                 bs_ref, wout_ref, bout_ref, o_ref, h_scr, v_scr, mix_scr, mu_scr, rs_scr,
                 *, n_half, gdim, chunk):
    j = pl.program_id(2)
    tm = v_scr.shape[0]
    tn = win_ref.shape[1]
    tn_out = wout_ref.shape[1]

    @pl.when(j == 0)
    def _():
        h_scr[...] = _norm_mod(x_ref[...], g_ref[...], sh_ref[...], sc_ref[...]).astype(BF16)

    @pl.when(j < n_half)
    def _():
        base = pl.multiple_of(j * tn, tn)

        def store(cols, acc):
            v_scr[:, pl.ds(base + cols.start, cols.stop - cols.start)] = jax.nn.gelu(acc + bin_ref[:, cols])
        _dot_chunks(h_scr, win_ref, store, chunk=2 * MXU_COLS)

    @pl.when(j == n_half - 1)
    def _():
        v = v_scr[...]
        mu = jnp.mean(v, axis=-1, keepdims=True)
        dv = v - mu
        mu_scr[...] = mu
        rs_scr[...] = lax.rsqrt(jnp.mean(dv * dv, axis=-1, keepdims=True) + EPS)

    @pl.when((j >= n_half) & (j < 2 * n_half))
    def _():
        jb = j - n_half
        base = pl.multiple_of(jb * tn, tn)
        causal = (lax.broadcasted_iota(jnp.int32, (chunk, chunk), 1)
                  <= lax.broadcasted_iota(jnp.int32, (chunk, chunk), 0))

        def store(cols, acc):
            u = jax.nn.gelu(acc + bin_ref[:, cols])
            vcols = pl.ds(base + cols.start, gdim)
            vn = (((v_scr[:, vcols] - mu_scr[...]) * rs_scr[...]) * lg_ref[:, cols]
                  + lb_ref[:, cols]).astype(BF16)
            gi = jb * (tn // gdim) + cols.start // gdim
            w = jnp.where(causal, ws_ref[gi], 0.0).astype(BF16)
            for r0 in range(0, tm, chunk):
                rows = slice(r0, r0 + chunk)
                mixed = _dot(w, vn[rows]) + bs_ref[gi]
                mix_scr[rows, vcols] = (u[rows] * mixed).astype(BF16)
        _dot_chunks(h_scr, win_ref, store, chunk=gdim)

    @pl.when(j >= 2 * n_half)
    def _():
        base = pl.multiple_of((j - 2 * n_half) * tn_out, tn_out)

        def store(cols, acc):
            xcols = pl.ds(base + cols.start, cols.stop - cols.start)
            o_ref[:, cols] = x_ref[:, xcols] + gate_ref[:, cols] * (acc + bout_ref[:, cols])
        _dot_chunks(mix_scr, wout_ref, store)


def _gmlp(x, g, shift, scale, gate, w_in, b_in, ln_g, ln_b, w_s, b_s, w_out, b_out):
    bsz, s, d = x.shape
    width = w_in.shape[1] // 2
    groups, chunk, _ = w_s.shape
    gdim = width // groups
    tm = _tile(s, 512)
    tn = _tile(width, 1024)
    tn_out = _tile(d, 512)
    n_half = width // tn
    n_out = d // tn_out
    assert tn % gdim == 0 and gdim % LANES == 0 and tm % chunk == 0

    def in_col(j):
        return jnp.where(j < n_half, j + n_half, jnp.minimum(j - n_half, n_half - 1))

    def u_col(j):
        return jnp.clip(j - n_half, 0, n_half - 1)

    def out_col(j):
        return jnp.clip(j - 2 * n_half, 0, n_out - 1)

    row = lambda bi, i, j: (bi, 0, 0)
    return pl.pallas_call(
        functools.partial(_gmlp_kernel, n_half=n_half, gdim=gdim, chunk=chunk),
        grid=(bsz, s // tm, 2 * n_half + n_out),
        in_specs=[
            pl.BlockSpec((None, tm, d), lambda bi, i, j: (bi, i, 0)),
            pl.BlockSpec((1, d), lambda bi, i, j: (0, 0)),
            pl.BlockSpec((None, 1, d), row),
            pl.BlockSpec((None, 1, d), row),
            pl.BlockSpec((None, 1, tn_out), lambda bi, i, j: (bi, 0, out_col(j))),
            pl.BlockSpec((None, d, tn), lambda bi, i, j: (in_col(j), 0, 0)),
            pl.BlockSpec((1, tn), lambda bi, i, j: (0, in_col(j))),
            pl.BlockSpec((1, tn), lambda bi, i, j: (0, u_col(j))),
            pl.BlockSpec((1, tn), lambda bi, i, j: (0, u_col(j))),
            pl.BlockSpec((groups, chunk, chunk), lambda bi, i, j: (0, 0, 0)),
            pl.BlockSpec((groups, chunk, 1), lambda bi, i, j: (0, 0, 0)),
            pl.BlockSpec((None, width, tn_out), lambda bi, i, j: (out_col(j), 0, 0)),
            pl.BlockSpec((1, tn_out), lambda bi, i, j: (0, out_col(j))),
        ],
        out_specs=pl.BlockSpec((None, tm, tn_out), lambda bi, i, j: (bi, i, out_col(j))),
        out_shape=jax.ShapeDtypeStruct((bsz, s, d), F32),
        scratch_shapes=[pltpu.VMEM((tm, d), BF16),
                        pltpu.VMEM((tm, width), F32),
                        pltpu.VMEM((tm, width), BF16),
                        pltpu.VMEM((tm, 1), F32), pltpu.VMEM((tm, 1), F32)],
        compiler_params=_params(("parallel", "parallel", "arbitrary")),
        name="gmlp",
    )(x, g.reshape(1, d), shift, scale, gate, _col_blocks(w_in, tn), b_in.reshape(1, 2 * width),
      ln_g.reshape(1, width), ln_b.reshape(1, width), w_s, b_s.reshape(groups, chunk, 1),
      _col_blocks(w_out, tn_out), b_out.reshape(1, d))


def _mm_res_kernel(a_ref, w_ref, x_ref, gate_ref, o_ref):
    def store(cols, acc):
        o_ref[:, cols] = x_ref[:, cols] + gate_ref[:, cols] * acc
    _dot_chunks(a_ref, w_ref, store)


def _mm_residual(a, w, x, gate):
    bsz, s, k = a.shape
    d = w.shape[1]
    tm = _tile(s, 512)
    tn = _tile(d, 2048)
    return pl.pallas_call(
        _mm_res_kernel,
        grid=(bsz, s // tm, d // tn),
        in_specs=[
            pl.BlockSpec((None, tm, k), lambda bi, i, j: (bi, i, 0)),
            pl.BlockSpec((k, tn), lambda bi, i, j: (0, j)),
            pl.BlockSpec((None, tm, tn), lambda bi, i, j: (bi, i, j)),
            pl.BlockSpec((None, 1, tn), lambda bi, i, j: (bi, 0, j)),
        ],
        out_specs=pl.BlockSpec((None, tm, tn), lambda bi, i, j: (bi, i, j)),
        out_shape=jax.ShapeDtypeStruct((bsz, s, d), F32),
        compiler_params=_params(("parallel", "parallel", "parallel")),
        name="matmul_residual",
    )(a, w, x, gate)


def _mlp_kernel(x_ref, g_ref, sh_ref, sc_ref, gate_ref, w1_ref, w2_ref, fg_ref, o_ref,
                h_scr, acc_scr, *, final_norm):
    f = pl.program_id(2)

    @pl.when(f == 0)
    def _():
        h_scr[...] = _norm_mod(x_ref[...], g_ref[...], sh_ref[...], sc_ref[...]).astype(BF16)
        acc_scr[...] = jnp.zeros_like(acc_scr)

    a = jnp.square(jnp.maximum(_dot(h_scr[...], w1_ref[...]), 0.0)).astype(BF16)
    acc_scr[...] += _dot(a, w2_ref[...])

    @pl.when(f == pl.num_programs(2) - 1)
    def _():
        xn = x_ref[...] + gate_ref[...] * acc_scr[...]
        if final_norm:
            xn = _rms(xn, fg_ref[...])
        o_ref[...] = xn


def _mlp(x, g, shift, scale, gate, w1, w2, layer, final_g, final_norm):
    bsz, s, d = x.shape
    dff = w1.shape[2]
    tm = _tile(s, 512)
    tf = _tile(dff, 1024)
    row = lambda bi, i, f: (bi, 0, 0)
    vec = pl.BlockSpec((1, d), lambda bi, i, f: (0, 0))
    xblk = pl.BlockSpec((None, tm, d), lambda bi, i, f: (bi, i, 0))
    return pl.pallas_call(
        functools.partial(_mlp_kernel, final_norm=final_norm),
        grid=(bsz, s // tm, dff // tf),
        in_specs=[
            xblk, vec,
            pl.BlockSpec((None, 1, d), row), pl.BlockSpec((None, 1, d), row),
            pl.BlockSpec((None, 1, d), row),
            pl.BlockSpec((None, d, tf), lambda bi, i, f: (layer, 0, f)),
            pl.BlockSpec((None, tf, d), lambda bi, i, f: (layer, f, 0)),
            vec,
        ],
        out_specs=xblk,
        out_shape=jax.ShapeDtypeStruct((bsz, s, d), F32),
        scratch_shapes=[pltpu.VMEM((tm, d), BF16), pltpu.VMEM((tm, d), F32)],
        compiler_params=_params(("parallel", "parallel", "arbitrary")),
        name="mlp",
    )(x, g.reshape(1, d), shift, scale, gate, w1, w2, final_g.reshape(1, d))


def _kv_kernel(x_ref, g_ref, sh_ref, sc_ref, w_ref, cos_ref, sin_ref, cmp_ref, rest_ref, h_scr,
               *, groups):
    j = pl.program_id(2)

    @pl.when(j == 0)
    def _():
        h_scr[...] = _norm_mod(x_ref[...], g_ref[...], sh_ref[...], sc_ref[...]).astype(BF16)

    def emit(dst, rotate):
        def store(cols, acc):
            for c0 in range(0, acc.shape[1], HEAD_DIM):
                a = acc[:, c0:c0 + HEAD_DIM]
                if rotate:
                    a = _rope(a, cos_ref[...], sin_ref[...])
                dst[(cols.start + c0) // HEAD_DIM] = a.astype(dst.dtype)
        _dot_chunks(h_scr, w_ref, store)

    @pl.when(j == 0)
    def _():
        emit(cmp_ref, True)

    @pl.when(j == 1)
    def _():
        emit(cmp_ref, False)

    @pl.when((j >= 2) & (j % 2 == 0))
    def _():
        emit(rest_ref, True)

    @pl.when((j >= 2) & (j % 2 == 1))
    def _():
        emit(rest_ref, False)


def _kv_proj(x, g, shift, scale, w, cos, sin, groups):
    bsz, s, d = x.shape
    tm = _tile(s, 1024)
    tn = groups * HEAD_DIM
    row = lambda bi, i, j: (bi, 0, 0)
    tab = pl.BlockSpec((tm, HEAD_DIM), lambda bi, i, j: (i, 0))
    return pl.pallas_call(
        functools.partial(_kv_kernel, groups=groups),
        grid=(bsz, s // tm, 2 * N_BRANCHES),
        in_specs=[
            pl.BlockSpec((None, tm, d), lambda bi, i, j: (bi, i, 0)),
            pl.BlockSpec((1, d), lambda bi, i, j: (0, 0)),
            pl.BlockSpec((None, 1, d), row), pl.BlockSpec((None, 1, d), row),
            pl.BlockSpec((None, d, tn), lambda bi, i, j: (j, 0, 0)),
            tab, tab,
        ],
        out_specs=[
            pl.BlockSpec((None, None, groups, tm, HEAD_DIM),
                         lambda bi, i, j: (bi, jnp.minimum(j, 1), 0, i, 0)),
            pl.BlockSpec((None, None, groups, tm, HEAD_DIM),
                         lambda bi, i, j: (bi, jnp.maximum(j - 2, 0), 0, i, 0)),
        ],
        out_shape=[jax.ShapeDtypeStruct((bsz, 2, groups, s, HEAD_DIM), F32),
                   jax.ShapeDtypeStruct((bsz, 4, groups, s, HEAD_DIM), BF16)],
        scratch_shapes=[pltpu.VMEM((tm, d), BF16)],
        compiler_params=_params(("parallel", "parallel", "arbitrary")),
        name="kv_proj",
    )(x, g.reshape(1, d), shift, scale, _col_blocks(w, tn), cos, sin)


def _qg_kernel(x_ref, g_ref, sh_ref, sc_ref, w_ref, cos_ref, sin_ref, q_ref, gate_ref, h_scr,
               *, heads_per_tile, n_q):
    j = pl.program_id(2)

    @pl.when(j == 0)
    def _():
        h_scr[...] = _norm_mod(x_ref[...], g_ref[...], sh_ref[...], sc_ref[...]).astype(BF16)

    @pl.when(j < n_q)
    def _():
        def store(cols, acc):
            for c0 in range(0, acc.shape[1], HEAD_DIM):
                a = _rope(acc[:, c0:c0 + HEAD_DIM], cos_ref[...], sin_ref[...]) * (HEAD_DIM ** -0.5)
                q_ref[:, cols.start + c0:cols.start + c0 + HEAD_DIM] = a.astype(BF16)
        _dot_chunks(h_scr, w_ref, store)

    @pl.when(j >= n_q)
    def _():
        def store(cols, acc):
            gate_ref[:, cols] = jax.nn.sigmoid(acc)
        _dot_chunks(h_scr, w_ref, store)


def _qg_proj(x, g, shift, scale, w_q, w_gate, cos, sin, tn):
    bsz, s, d = x.shape
    nq_cols, ng_cols = w_q.shape[1], w_gate.shape[1]
    n_q, n_g = nq_cols // tn, ng_cols // tn
    tm = _tile(s, 1024)
    row = lambda bi, i, j: (bi, 0, 0)
    tab = pl.BlockSpec((tm, HEAD_DIM), lambda bi, i, j: (i, 0))
    w = jnp.concatenate([w_q, w_gate], axis=1)
    return pl.pallas_call(
        functools.partial(_qg_kernel, heads_per_tile=tn // HEAD_DIM, n_q=n_q),
        grid=(bsz, s // tm, n_q + n_g),
        in_specs=[
            pl.BlockSpec((None, tm, d), lambda bi, i, j: (bi, i, 0)),
            pl.BlockSpec((1, d), lambda bi, i, j: (0, 0)),
            pl.BlockSpec((None, 1, d), row), pl.BlockSpec((None, 1, d), row),
            pl.BlockSpec((None, d, tn), lambda bi, i, j: (j, 0, 0)),
            tab, tab,
        ],
        out_specs=[
            pl.BlockSpec((None, tm, tn), lambda bi, i, j: (bi, i, jnp.minimum(j, n_q - 1))),
            pl.BlockSpec((None, tm, tn), lambda bi, i, j: (bi, i, jnp.maximum(j - n_q, 0))),
        ],
        out_shape=[jax.ShapeDtypeStruct((bsz, s, nq_cols), BF16),
                   jax.ShapeDtypeStruct((bsz, s, ng_cols), F32)],
        scratch_shapes=[pltpu.VMEM((tm, d), BF16)],
        compiler_params=_params(("parallel", "parallel", "arbitrary")),
        name="qg_proj",
    )(x, g.reshape(1, d), shift, scale, _col_blocks(w, tn), cos, sin)


def _compress_kernel(kv_ref, pos_ref, w1_ref, w2_ref, w2t_ref, o_ref, ot_ref):
    n_rows = kv_ref.shape[0] // CMP_STRIDE

    def half_block(p0):
        total = None
        for p in range(0, CMP_STRIDE, 2):
            a = jnp.concatenate(
                [kv_ref[pl.ds(p + q, n_rows, stride=CMP_STRIDE), :] + pos_ref[p0 + p + q:p0 + p + q + 1, :]
                 for q in range(2)], axis=1).astype(BF16)
            part = _dot(a, w1_ref[(p0 + p) * HEAD_DIM:(p0 + p + 2) * HEAD_DIM, :])
            total = part if total is None else total + part
        return total

    first = half_block(0)
    second = half_block(CMP_STRIDE)
    pre = first + pltpu.roll(second, n_rows - 1, 0)
    hidden = jax.nn.gelu(pre).astype(BF16)
    o_ref[...] = _dot(hidden, w2_ref[...]).astype(BF16)
    ot_ref[...] = _dot_nt(w2t_ref[...], hidden).astype(BF16)


def _compress(kv_cmp, pos, w1, w2):
    bsz, _, groups, s, _ = kv_cmp.shape
    n_rows = s // CMP_STRIDE
    hidden = w1.shape[2]
    return pl.pallas_call(
        _compress_kernel,
        grid=(bsz, 2, groups),
        in_specs=[
            pl.BlockSpec((None, None, None, s, HEAD_DIM), lambda bi, t, gi: (bi, t, gi, 0, 0)),
            pl.BlockSpec((None, CMP_BLOCK, HEAD_DIM), lambda bi, t, gi: (t, 0, 0)),
            pl.BlockSpec((None, CMP_BLOCK * HEAD_DIM, hidden), lambda bi, t, gi: (t, 0, 0)),
            pl.BlockSpec((None, hidden, HEAD_DIM), lambda bi, t, gi: (t, 0, 0)),
            pl.BlockSpec((None, HEAD_DIM, hidden), lambda bi, t, gi: (t, 0, 0)),
        ],
        out_specs=[pl.BlockSpec((None, None, None, n_rows, HEAD_DIM), lambda bi, t, gi: (bi, t, gi, 0, 0)),
                   pl.BlockSpec((None, None, None, HEAD_DIM, n_rows), lambda bi, t, gi: (bi, t, gi, 0, 0))],
        out_shape=[jax.ShapeDtypeStruct((bsz, 2, groups, n_rows, HEAD_DIM), BF16),
                   jax.ShapeDtypeStruct((bsz, 2, groups, HEAD_DIM, n_rows), BF16)],
        compiler_params=_params(("parallel", "parallel", "parallel")),
        name="compress",
    )(kv_cmp, pos, w1, w2, jnp.swapaxes(w2, 1, 2))


def _masked_softmax(s, mask, axis):
    s = jnp.where(mask, s, NEG_INF)
    m = jnp.max(s, axis=axis, keepdims=True)
    e = jnp.where(mask, jnp.exp(s - m), 0.0)
    den = jnp.sum(e, axis=axis, keepdims=True)
    return e / jnp.maximum(den, 1e-30)


def _attn_kernel(q_ref, gate_ref, kc_ref, vct_ref, ks_ref, vst_ref, kw_ref, vwt_ref, ovt_ref,
                 o_ref, q4_scr, sel_scr, s_scr, m_scr, acc_scr, oc_scr, ow_scr,
                 *, hg, n_cmp, n_slc, n_sel):
    tq = ATTN_TILE
    tk = SLC_TILE
    i = pl.program_id(2)
    t0 = i * tq
    for h in range(hg):
        q4_scr[h * tq:(h + 1) * tq, :] = q_ref[:, h * HEAD_DIM:(h + 1) * HEAD_DIM]
    q4 = q4_scr[...]
    tpos_l = t0 + lax.broadcasted_iota(jnp.int32, (1, tq), 1)
    tpos4_l = jnp.concatenate([tpos_l] * hg, axis=1)

    def heads(a, h):
        return a[:, h * tq:(h + 1) * tq]

    n_idx = lax.broadcasted_iota(jnp.int32, (kc_ref.shape[0], 1), 0)
    valid = (n_idx * CMP_STRIDE + (CMP_BLOCK - 1) <= tpos4_l) & (n_idx < n_cmp)
    p_cmp = _masked_softmax(_dot_nt(kc_ref[...], q4), valid, 0).astype(BF16)
    oc_scr[...] = _dot(vct_ref[...], p_cmp)

    p_heads = jnp.concatenate([heads(p_cmp, h) for h in range(hg)], axis=0)
    imp = _dot(ovt_ref[...], p_heads)[:n_slc]
    blk = lax.broadcasted_iota(jnp.int32, (n_slc, 1), 0)
    cur = tpos_l // SEL_BLOCK
    forced = (blk == 0) | (blk == cur) | (blk == cur - 1)
    imp = jnp.where(blk * SEL_BLOCK <= tpos_l, imp + jnp.where(forced, FORCE_BONUS, 0.0), NEG_INF)
    rank = jnp.zeros((n_slc, tq), jnp.int32)
    for c in range(n_slc):
        row = imp[c:c + 1, :]
        beats = (row > imp) | ((row == imp) & (c < blk))
        rank = rank + jnp.where(beats, 1, 0)
    sel_scr[...] = jnp.where(rank < n_sel, 1.0, 0.0)

    def with_ones(vt):
        return jnp.concatenate([vt, jnp.ones(vt.shape, BF16)], axis=0)

    def normalised(acc):
        return acc[:HEAD_DIM] / jnp.maximum(acc[HEAD_DIM:], 1e-30)

    def sublane_tile_max(sh):
        return jnp.max(sh.reshape(sh.shape[0] // SUBLANES, SUBLANES, sh.shape[1]), axis=0)

    wk = WINDOW + tq
    kw0 = pl.multiple_of(jnp.maximum(t0 - WINDOW, 0), tq)
    kpos = kw0 + lax.broadcasted_iota(jnp.int32, (wk, 1), 0)
    bias = jnp.where((kpos <= tpos_l) & (kpos > tpos_l - WINDOW), 0.0, NEG_INF)
    s4 = _dot_nt(kw_ref[pl.ds(kw0, wk), :], q4)
    e = []
    for h in range(hg):
        sh = heads(s4, h) + bias
        m = jnp.max(sublane_tile_max(sh), axis=0, keepdims=True)
        e.append(jnp.exp(sh - m).astype(BF16))
    ow_scr[...] = normalised(_dot(with_ones(vwt_ref[:, pl.ds(kw0, wk)]), jnp.concatenate(e, axis=1)))

    n_tiles = i // (tk // tq) + 1
    blocks_per_tile = tk // SEL_BLOCK
    m_scr[...] = jnp.full_like(m_scr, NEG_INF)

    def scores(kt, carry):
        k0 = pl.multiple_of(kt * tk, tk)
        sel = sel_scr[pl.ds(pl.multiple_of(kt * blocks_per_tile, blocks_per_tile), blocks_per_tile), :]
        bias = []
        for jb in range(blocks_per_tile):
            kpos = k0 + jb * SEL_BLOCK + lax.broadcasted_iota(jnp.int32, (SEL_BLOCK, 1), 0)
            bias.append(jnp.where((sel[jb:jb + 1, :] > 0.5) & (kpos <= tpos_l), 0.0, NEG_INF))
        bias = jnp.concatenate(bias, axis=0)
        s4 = _dot_nt(ks_ref[pl.ds(k0, tk), :], q4)
        for h in range(hg):
            cols = slice(h * tq, (h + 1) * tq)
            sh = (heads(s4, h) + bias) * LOG2_E
            s_scr[pl.ds(k0, tk), cols] = sh
            m_scr[:, cols] = jnp.maximum(m_scr[:, cols], sublane_tile_max(sh))
        return carry

    lax.fori_loop(0, n_tiles, scores, 0)
    m_scr[...] = jnp.broadcast_to(jnp.max(m_scr[...], axis=0, keepdims=True), m_scr.shape)
    acc_scr[...] = jnp.zeros_like(acc_scr)

    def weighted(kt, carry):
        k0 = pl.multiple_of(kt * tk, tk)
        s = s_scr[pl.ds(k0, tk), :].reshape(tk // SUBLANES, SUBLANES, hg * tq)
        e = jnp.exp2(s - m_scr[...]).reshape(tk, hg * tq).astype(BF16)
        acc_scr[...] += _dot(with_ones(vst_ref[:, pl.ds(k0, tk)]), e)
        return carry

    lax.fori_loop(0, n_tiles, weighted, 0)

    o_slc = normalised(acc_scr[...])
    gates_t = gate_ref[...].T
    for h in range(hg):
        c0 = h * N_BRANCHES
        o = (gates_t[c0:c0 + 1] * heads(oc_scr, h) + gates_t[c0 + 1:c0 + 2] * heads(o_slc, h)
             + gates_t[c0 + 2:c0 + 3] * heads(ow_scr, h))
        o_ref[:, h * HEAD_DIM:(h + 1) * HEAD_DIM] = o.T.astype(BF16)


def _attention(q, gates, kv_cmp, kv_cmp_t, kv_rest, v_rest_t, hg):
    bsz, s, _ = q.shape
    groups = kv_rest.shape[2]
    n_rows = kv_cmp.shape[3]
    n_cmp = n_rows - 1
    n_slc = s // SEL_BLOCK
    n_sel = min(N_SELECT, n_slc)
    tq = ATTN_TILE
    assert n_rows == LANES and n_slc <= LANES and n_slc % (SLC_TILE // SEL_BLOCK) == 0
    assert s % SLC_TILE == 0 and s >= WINDOW + tq

    ci = np.arange(LANES)[None, :]
    sj = np.arange(LANES)[:, None]
    overlap_t = ((ci * CMP_STRIDE <= sj * SEL_BLOCK + SEL_BLOCK - 1)
                 & (ci * CMP_STRIDE + CMP_BLOCK - 1 >= sj * SEL_BLOCK) & (ci < n_cmp) & (sj < n_slc))
    overlap_t = jnp.asarray(np.tile(overlap_t, (1, hg)), BF16)

    def k_spec(which):
        return pl.BlockSpec((None, None, None, s, HEAD_DIM), lambda bi, gi, i: (bi, which, gi, 0, 0))

    def vt_spec(which):
        return pl.BlockSpec((None, None, None, HEAD_DIM, s), lambda bi, gi, i: (bi, which, gi, 0, 0))

    def cmp_spec(which):
        return pl.BlockSpec((None, None, None, n_rows, HEAD_DIM), lambda bi, gi, i: (bi, which, gi, 0, 0))

    qblk = pl.BlockSpec((None, tq, hg * HEAD_DIM), lambda bi, gi, i: (bi, i, gi))
    return pl.pallas_call(
        functools.partial(_attn_kernel, hg=hg, n_cmp=n_cmp, n_slc=n_slc, n_sel=n_sel),
        grid=(bsz, groups, s // tq),
        in_specs=[
            qblk,
            pl.BlockSpec((None, tq, LANES), lambda bi, gi, i: (bi, i, gi)),
            cmp_spec(0), cmp_spec(1),
            k_spec(0), vt_spec(0), k_spec(2), vt_spec(1),
            pl.BlockSpec((LANES, hg * LANES), lambda bi, gi, i: (0, 0)),
        ],
        out_specs=qblk,
        out_shape=jax.ShapeDtypeStruct(q.shape, BF16),
        scratch_shapes=[pltpu.VMEM((hg * tq, HEAD_DIM), BF16),
                        pltpu.VMEM((n_slc, tq), F32),
                        pltpu.VMEM((s, hg * tq), F32),
                        pltpu.VMEM((SUBLANES, hg * tq), F32),
                        pltpu.VMEM((2 * HEAD_DIM, hg * tq), F32),
                        pltpu.VMEM((HEAD_DIM, hg * tq), F32),
                        pltpu.VMEM((HEAD_DIM, hg * tq), F32)],
        compiler_params=_params(("parallel", "parallel", "arbitrary")),
        name="nsa_attention",
    )(q, gates, kv_cmp, kv_cmp_t, kv_rest, v_rest_t, kv_rest, v_rest_t, overlap_t)


def _rope_tables(s):
    freqs = ROPE_THETA ** (-jnp.arange(HALF_HEAD, dtype=F32) / HALF_HEAD)
    ang = jnp.arange(s).astype(F32)[:, None] * freqs[None, :]
    cos, sin = jnp.cos(ang), jnp.sin(ang)
    return jnp.concatenate([cos, cos], axis=1), jnp.concatenate([-sin, sin], axis=1)


def kernel(x, c, mod_w, mod_b, norm_g, mlp_w1, mlp_w2, a_w_in, a_b_in, a_ln_g, a_ln_b, a_w_s, a_b_s, a_w_out, a_b_out, kv_norm_g, kv_mod_w, kv_mod_b, w_kv, cmp_pos_k, cmp_w1_k, cmp_w2_k, cmp_pos_v, cmp_w1_v, cmp_w2_v, b_w_qg, b_w_o, final_g):
    bsz, s, d = x.shape
    depth = mod_w.shape[0]
    n_a = a_w_in.shape[0]
    groups = w_kv.shape[1] // (2 * N_BRANCHES * HEAD_DIM)
    heads = b_w_o.shape[1] // HEAD_DIM
    hg = heads // groups
    assert hg * N_BRANCHES <= LANES

    mod = _cond_matmul(c, mod_w.reshape(depth * 2, d, 3 * d), mod_b.reshape(depth * 2, 3 * d))
    mod = mod.reshape(depth, 2, bsz, 3, 1, d)
    kv_mod = _cond_matmul(c, kv_mod_w[None], kv_mod_b[None]).reshape(bsz, 2, 1, d)
    cos, sin = _rope_tables(s)
    w1_all, w2_all = mlp_w1.astype(BF16), mlp_w2.astype(BF16)

    for layer in range(depth):
        shift, scale, gate = (mod[layer, 0, :, t] for t in range(3))
        if layer < n_a:
            x = _gmlp(x, norm_g[layer, 0], shift, scale, gate,
                      a_w_in[layer].astype(BF16), a_b_in[layer], a_ln_g[layer], a_ln_b[layer],
                      a_w_s[layer], a_b_s[layer], a_w_out[layer].astype(BF16), a_b_out[layer])
        else:
            j = layer - n_a
            if layer == n_a:
                kv_cmp, kv_rest = _kv_proj(x, kv_norm_g, kv_mod[:, 0], kv_mod[:, 1],
                                           w_kv.astype(BF16), cos, sin, groups)
                pos = jnp.stack([cmp_pos_k, cmp_pos_v])
                kv_cmp, kv_cmp_t = _compress(kv_cmp, pos,
                                             jnp.stack([cmp_w1_k, cmp_w1_v]).astype(BF16),
                                             jnp.stack([cmp_w2_k, cmp_w2_v]).astype(BF16))
                v_rest_t = jnp.swapaxes(kv_rest[:, 1::2], 3, 4)
            w_qg = b_w_qg[j]
            w_q = w_qg[:, :heads * HEAD_DIM]
            w_gate = w_qg[:, heads * HEAD_DIM:].reshape(d, groups, hg * N_BRANCHES)
            w_gate = jnp.pad(w_gate, ((0, 0), (0, 0), (0, LANES - hg * N_BRANCHES)))
            w_gate = w_gate.reshape(d, groups * LANES)
            tn = hg * HEAD_DIM
            pad = (-w_gate.shape[1]) % tn
            w_gate = jnp.pad(w_gate, ((0, 0), (0, pad)))
            q, gates = _qg_proj(x, norm_g[layer, 0], shift, scale,
                                w_q.astype(BF16), w_gate.astype(BF16), cos, sin, tn)
            o = _attention(q, gates, kv_cmp, kv_cmp_t, kv_rest, v_rest_t, hg)
            x = _mm_residual(o, b_w_o[j].astype(BF16), x, gate)

        shift, scale, gate = (mod[layer, 1, :, t] for t in range(3))
        x = _mlp(x, norm_g[layer, 1], shift, scale, gate, w1_all, w2_all, layer,
                 final_g, final_norm=(layer == depth - 1))
    return x
```
